```python
import math
import jax, jax.numpy as jnp
from jax import lax
import numpy as np

D_MODEL = 2048
BATCH = 2
SEQ = 16384
DEPTH = 2
DEC_BATCH = 4
DEC_SEQ = 4096
PAST_LEN = 128

EPS = 1e-6
ATTN_HEADS = 8
Q_LORA = 512
KV_LORA = 256
NOPE_DIM = 128
ROPE_DIM = 64
V_DIM = 128
ROPE_BASE = 10000.0
Q_BLOCK = 128
SSD_HEADS = 8
SSD_HEAD_DIM = 64
SSD_INNER = SSD_HEADS * SSD_HEAD_DIM
SSD_GROUPS = 2
SSD_STATE = 128
SSD_CHUNK = 128
D_CONV = 4
CONV_PAD_LEFT = D_CONV // 2
CONV_DIM = SSD_INNER + 2 * SSD_GROUPS * SSD_STATE
POOL_WINDOWS = (2, 4, 8, 16)
POOL_GROUP = 128
POOL_DIM = len(POOL_WINDOWS) * POOL_GROUP
ATTN_OUT = ATTN_HEADS * V_DIM
MIX_DIM = ATTN_OUT + SSD_INNER + POOL_DIM
IN_DIM = Q_LORA + KV_LORA + ROPE_DIM + SSD_INNER + CONV_DIM + 2 * SSD_HEADS + POOL_DIM
IN_SPLITS = (Q_LORA,
             Q_LORA + KV_LORA,
             Q_LORA + KV_LORA + ROPE_DIM,
             Q_LORA + KV_LORA + ROPE_DIM + SSD_INNER,
             Q_LORA + KV_LORA + ROPE_DIM + SSD_INNER + CONV_DIM,
             Q_LORA + KV_LORA + ROPE_DIM + SSD_INNER + CONV_DIM + 2 * SSD_HEADS)
PEER_HEADS = 8
N_KEYS = 128
N_EXPERTS = N_KEYS * N_KEYS
PEER_QUERY = 256
PEER_HALF = PEER_QUERY // 2
PEER_TOPK = 16
PEER_BLOCK = 128

kernel_name = 'hybrid_bidir_mla_ssd_pool_peer_encoder'


def rms_norm(x, g):
    xf = x.astype(jnp.float32)
    y = xf * lax.rsqrt(jnp.mean(xf * xf, axis=-1, keepdims=True) + EPS)
    return (y * g.astype(jnp.float32)).astype(x.dtype)


def rope_tables(L):
    inv = 1.0 / (ROPE_BASE ** (jnp.arange(0, ROPE_DIM, 2, dtype=jnp.float32) / ROPE_DIM))
    ang = jnp.arange(L, dtype=jnp.float32)[:, None] * inv[None, :]
    return jnp.cos(ang), jnp.sin(ang)


def apply_rope(x, cos, sin):
    half = x.shape[-1] // 2
    x1, x2 = x[..., :half], x[..., half:]
    cos = cos.astype(x.dtype)
    sin = sin.astype(x.dtype)
    return jnp.concatenate([x1 * cos - x2 * sin, x2 * cos + x1 * sin], axis=-1)


def mla_attention(q_nope, q_rope, k_nope, k_rope, v):
    b, L, h, _ = q_nope.shape
    nb = L // Q_BLOCK
    scale = 1.0 / math.sqrt(NOPE_DIM + ROPE_DIM)

    def to_blocks(t):
        return jnp.moveaxis(t.reshape((b, nb, Q_BLOCK) + t.shape[2:]), 1, 0)

    def attend(blk):
        qn, qr = blk
        s = jnp.einsum('bqhd,bkhd->bhqk', qn, k_nope) + jnp.einsum('bqhd,bkd->bhqk', qr, k_rope)
        w = jax.nn.softmax(s.astype(jnp.float32) * scale, axis=-1).astype(v.dtype)
        return jnp.einsum('bhqk,bkhd->bqhd', w, v)

    o = lax.map(attend, (to_blocks(q_nope), to_blocks(q_rope)))
    return jnp.moveaxis(o, 0, 1).reshape(b, L, h * V_DIM)


def centred_depthwise_conv(u, w, bias):
    y = lax.conv_general_dilated(
        u, w[:, None, :].astype(u.dtype), window_strides=(1,),
        padding=[(CONV_PAD_LEFT, D_CONV - 1 - CONV_PAD_LEFT)],
        dimension_numbers=('NWC', 'WIO', 'NWC'), feature_group_count=u.shape[-1])
    return y + bias.astype(u.dtype)


def ssd_chunked(x, dt, A, B, C):
    b, L, h, p = x.shape
    n = B.shape[-1]
    nc = L // SSD_CHUNK
    x = x.reshape(b, nc, SSD_CHUNK, h, p)
    dt = dt.reshape(b, nc, SSD_CHUNK, h)
    B = B.reshape(b, nc, SSD_CHUNK, h, n)
    C = C.reshape(b, nc, SSD_CHUNK, h, n)
    a_cum = jnp.cumsum(dt * A, axis=2)
    xdt = x * dt[..., None]
    seg = a_cum[:, :, :, None, :] - a_cum[:, :, None, :, :]
    lower = jnp.tril(jnp.ones((SSD_CHUNK, SSD_CHUNK), dtype=bool))[None, None, :, :, None]
    decay = jnp.exp(jnp.where(lower, seg, -jnp.inf))
    scores = jnp.einsum('bclhn,bcshn->bclsh', C, B) * decay
    y_diag = jnp.einsum('bclsh,bcshp->bclhp', scores, xdt)
    decay_to_end = jnp.exp(a_cum[:, :, -1:, :] - a_cum)
    chunk_states = jnp.einsum('bclhn,bclh,bclhp->bchpn', B, decay_to_end, xdt)
    chunk_decay = jnp.exp(a_cum[:, :, -1, :])

    def step(state, inp):
        s_c, d_c = inp
        return state * d_c[:, :, None, None] + s_c, state

    _, states_in = lax.scan(step, jnp.zeros((b, h, p, n), jnp.float32),
                            (jnp.moveaxis(chunk_states, 1, 0), jnp.moveaxis(chunk_decay, 1, 0)))
    states_in = jnp.moveaxis(states_in, 0, 1)
    y_off = jnp.einsum('bclhn,bchpn,bclh->bclhp', C, states_in, jnp.exp(a_cum))
    return (y_diag + y_off).reshape(b, L, h, p)


def ssd_mixer(z, xbc, dt_raw, p, i):
    b, L, _ = z.shape
    f32 = jnp.float32
    xbc = jax.nn.silu(centred_depthwise_conv(xbc, p['conv_w'][i], p['conv_b'][i])).astype(f32)
    hpg = SSD_HEADS // SSD_GROUPS
    xs = xbc[..., :SSD_INNER].reshape(b, L, SSD_HEADS, SSD_HEAD_DIM)
    gn = SSD_GROUPS * SSD_STATE
    Bm = jnp.repeat(xbc[..., SSD_INNER:SSD_INNER + gn].reshape(b, L, SSD_GROUPS, SSD_STATE), hpg, axis=2)
    Cm = jnp.repeat(xbc[..., SSD_INNER + gn:].reshape(b, L, SSD_GROUPS, SSD_STATE), hpg, axis=2)
    dt_raw = dt_raw.astype(f32)
    dt_f = jax.nn.softplus(dt_raw[..., :SSD_HEADS] + p['dt_bias_fwd'][i].astype(f32))
    dt_b = jax.nn.softplus(dt_raw[..., SSD_HEADS:] + p['dt_bias_bwd'][i].astype(f32))
    A_f = -jnp.exp(p['a_log_fwd'][i].astype(f32))
    A_b = -jnp.exp(p['a_log_bwd'][i].astype(f32))
    y_f = ssd_chunked(xs, dt_f, A_f, Bm, Cm)
    rev = lambda t: jnp.flip(t, axis=1)
    y_b = rev(ssd_chunked(rev(xs), rev(dt_b), A_b, rev(Bm), rev(Cm)))
    y = (y_f + y_b + xs * p['d_skip'][i].astype(f32)[:, None]).reshape(b, L, SSD_INNER)
    return rms_norm(y * jax.nn.silu(z.astype(f32)), p['ssd_norm_g'][i]).astype(z.dtype)


def multiscale_pool(u, pool_w, pool_scale):
    b, L, _ = u.shape
    uf = u.astype(jnp.float32).reshape(b, L, len(POOL_WINDOWS), POOL_GROUP)
    csum = jnp.concatenate([jnp.zeros((b, 1) + uf.shape[2:], jnp.float32), jnp.cumsum(uf, axis=1)], axis=1)
    t = jnp.arange(L)
    outs = []
    for gi, w in enumerate(POOL_WINDOWS):
        lo = jnp.clip(t - w // 2, 0, L)
        hi = jnp.clip(t + w // 2, 0, L)
        cg = csum[:, :, gi]
        mean = (cg[:, hi] - cg[:, lo]) / (hi - lo).astype(jnp.float32)[:, None]
        outs.append(mean - uf[:, :, gi])
    d = jnp.stack(outs, axis=2).astype(u.dtype)
    y = jnp.einsum('blgc,gcd->blgd', d, pool_w)
    return y.reshape(b, L, POOL_DIM) * pool_scale


def token_mixer(h, p, i):
    b, L, _ = h.shape
    proj = h @ p['w_in'][i]
    cq, ckv, k_rope, z, xbc, dt_raw, pool_in = jnp.split(proj, IN_SPLITS, axis=-1)
    q = (rms_norm(cq, p['q_a_norm_g'][i]) @ p['w_q_b'][i]).reshape(b, L, ATTN_HEADS, NOPE_DIM + ROPE_DIM)
    kv = (rms_norm(ckv, p['kv_a_norm_g'][i]) @ p['w_kv_b'][i]).reshape(b, L, ATTN_HEADS, NOPE_DIM + V_DIM)
    cos, sin = rope_tables(L)
    q_rope = apply_rope(q[..., NOPE_DIM:], cos[:, None, :], sin[:, None, :])
    k_rope = apply_rope(k_rope, cos, sin)
    attn = mla_attention(q[..., :NOPE_DIM], q_rope, kv[..., :NOPE_DIM], k_rope, kv[..., NOPE_DIM:])
    ssd = ssd_mixer(z, xbc, dt_raw, p, i)
    pool = multiscale_pool(pool_in, p['pool_w'][i], p['pool_scale'][i])
    return jnp.concatenate([attn, ssd, pool], axis=-1) @ p['w_out'][i]


def peer_ffn(h, wq, keys, u_tab, v_tab):
    b, L, D = h.shape
    T = b * L
    xt = h.reshape(T, D)
    q = (xt @ wq).reshape(T, PEER_HEADS, 2, PEER_HALF)
    s = jnp.einsum('thpd,hpkd->thpk', q, keys).astype(jnp.float32)
    s1, i1 = lax.top_k(s[:, :, 0], PEER_TOPK)
    s2, i2 = lax.top_k(s[:, :, 1], PEER_TOPK)
    cand = (s1[..., :, None] + s2[..., None, :]).reshape(T, PEER_HEADS, PEER_TOPK * PEER_TOPK)
    cidx = (i1[..., :, None] * N_KEYS + i2[..., None, :]).reshape(T, PEER_HEADS, PEER_TOPK * PEER_TOPK)
    top_s, pos = lax.top_k(cand, PEER_TOPK)
    idx = jnp.take_along_axis(cidx, pos, axis=-1)
    g = jax.nn.softmax(top_s, axis=-1).astype(h.dtype)
    nb = T // PEER_BLOCK
    HK = PEER_HEADS * PEER_TOPK

    def experts(blk):
        xb, ib, gb = blk
        u = jnp.take(u_tab, ib, axis=0)
        a = jax.nn.gelu(jnp.einsum('td,tkd->tk', xb, u), approximate=False)
        vv = jnp.take(v_tab, ib, axis=0)
        return jnp.einsum('tk,tkd->td', a * gb, vv)

    out = lax.map(experts, (xt.reshape(nb, PEER_BLOCK, D), idx.reshape(nb, PEER_BLOCK, HK), g.reshape(nb, PEER_BLOCK, HK)))
    return out.reshape(b, L, D)


def encoder(x, c, p, final_norm_g):
    cs = jax.nn.silu(c)
    for i in range(DEPTH):
        mod = (cs @ p['mod_w'][i] + p['mod_b'][i])[:, None, :]
        sh1, sc1, g1, sh2, sc2, g2 = jnp.split(mod, 6, axis=-1)
        h = rms_norm(x, p['norm1_g'][i]) * (1 + sc1) + sh1
        x = x + g1 * token_mixer(h, p, i)
        h = rms_norm(x, p['norm2_g'][i]) * (1 + sc2) + sh2
        x = x + g2 * peer_ffn(h, p['peer_wq'][i], p['peer_keys'][i], p['peer_u'][i], p['peer_v'][i])
    return rms_norm(x, final_norm_g)


def setup_inputs(seed: int = 0) -> dict:
    key = jax.random.key(seed)
    ks = jax.random.split(key, 32)
    f32 = jnp.float32
    L = DEPTH

    def nrm(k, shape, scale):
        return jax.random.normal(k, shape, f32) * scale

    def gain(k, shape):
        return 1.0 + 0.05 * jax.random.normal(k, shape, f32)

    def dt_bias(k):
        dt = jnp.exp(jax.random.uniform(k, (L, SSD_HEADS), f32, math.log(1e-3), math.log(1e-1)))
        return dt + jnp.log(-jnp.expm1(-dt))

    return {
        'x_prompt': nrm(ks[0], (BATCH, SEQ, D_MODEL), 1.0),
        'x_sample': nrm(ks[1], (DEC_BATCH, DEC_SEQ, D_MODEL), 1.0),
        'c_prompt': nrm(ks[2], (BATCH, D_MODEL), 1.0),
        'c_sample': nrm(ks[3], (DEC_BATCH, D_MODEL), 1.0),
        'mod_w': nrm(ks[4], (L, D_MODEL, 6 * D_MODEL), 0.5 * D_MODEL ** -0.5),
        'mod_b': nrm(ks[5], (L, 6 * D_MODEL), 0.02),
        'norm1_g': gain(ks[6], (L, D_MODEL)),
        'norm2_g': gain(ks[7], (L, D_MODEL)),
        'w_in': nrm(ks[8], (L, D_MODEL, IN_DIM), D_MODEL ** -0.5),
        'q_a_norm_g': gain(ks[9], (L, Q_LORA)),
        'w_q_b': nrm(ks[10], (L, Q_LORA, ATTN_HEADS * (NOPE_DIM + ROPE_DIM)), Q_LORA ** -0.5),
        'kv_a_norm_g': gain(ks[11], (L, KV_LORA)),
        'w_kv_b': nrm(ks[12], (L, KV_LORA, ATTN_HEADS * (NOPE_DIM + V_DIM)), KV_LORA ** -0.5),
        'conv_w': nrm(ks[13], (L, D_CONV, CONV_DIM), D_CONV ** -0.5),
        'conv_b': nrm(ks[14], (L, CONV_DIM), 0.02),
        'a_log_fwd': jnp.log(jax.random.uniform(ks[15], (L, SSD_HEADS), f32, 1.0, 16.0)),
        'a_log_bwd': jnp.log(jax.random.uniform(ks[16], (L, SSD_HEADS), f32, 1.0, 16.0)),
        'dt_bias_fwd': dt_bias(ks[17]),
        'dt_bias_bwd': dt_bias(ks[18]),
        'd_skip': gain(ks[19], (L, SSD_HEADS)),
        'ssd_norm_g': gain(ks[20], (L, SSD_INNER)),
        'pool_w': nrm(ks[21], (L, len(POOL_WINDOWS), POOL_GROUP, POOL_GROUP), POOL_GROUP ** -0.5),
        'pool_scale': gain(ks[22], (L, POOL_DIM)),
        'w_out': nrm(ks[23], (L, MIX_DIM, D_MODEL), MIX_DIM ** -0.5),
        'peer_wq': nrm(ks[24], (L, D_MODEL, PEER_HEADS * PEER_QUERY), D_MODEL ** -0.5),
        'peer_keys': nrm(ks[25], (L, PEER_HEADS, 2, N_KEYS, PEER_HALF), PEER_HALF ** -0.5),
        'peer_u': nrm(ks[26], (L, N_EXPERTS, D_MODEL), D_MODEL ** -0.5),
        'peer_v': nrm(ks[27], (L, N_EXPERTS, D_MODEL), 1.0),
        'final_norm_g': gain(ks[28], (D_MODEL,)),
    }


def reference(x_prompt, x_sample, c_prompt, c_sample, mod_w, mod_b, norm1_g, norm2_g, w_in, q_a_norm_g, w_q_b,
              kv_a_norm_g, w_kv_b, conv_w, conv_b, a_log_fwd, a_log_bwd, dt_bias_fwd, dt_bias_bwd, d_skip,
              ssd_norm_g, pool_w, pool_scale, w_out, peer_wq, peer_keys, peer_u, peer_v, final_norm_g):
    layers = dict(mod_w=mod_w, mod_b=mod_b, norm1_g=norm1_g, norm2_g=norm2_g, w_in=w_in,
                  q_a_norm_g=q_a_norm_g, w_q_b=w_q_b, kv_a_norm_g=kv_a_norm_g, w_kv_b=w_kv_b,
                  conv_w=conv_w, conv_b=conv_b, a_log_fwd=a_log_fwd, a_log_bwd=a_log_bwd,
                  dt_bias_fwd=dt_bias_fwd, dt_bias_bwd=dt_bias_bwd, d_skip=d_skip, ssd_norm_g=ssd_norm_g,
                  pool_w=pool_w, pool_scale=pool_scale, w_out=w_out, peer_wq=peer_wq, peer_keys=peer_keys,
                  peer_u=peer_u, peer_v=peer_v)
    y_prompt = encoder(x_prompt, c_prompt, layers, final_norm_g)
    y_sample = encoder(x_sample, c_sample, layers, final_norm_g)
    return (y_prompt, y_sample)
```

```python
import functools
import math

import jax
import jax.numpy as jnp
from jax import lax
from jax.experimental import pallas as pl
from jax.experimental.pallas import tpu as pltpu

F32 = jnp.float32
BF16 = jnp.bfloat16
I32 = jnp.int32

D_MODEL = 2048
DEPTH = 2
EPS = 1e-6
ATTN_HEADS = 8
Q_LORA = 512
KV_LORA = 256
NOPE_DIM = 128
ROPE_DIM = 64
V_DIM = 128
ROPE_BASE = 10000.0
QK_PAD = 256
SSD_HEADS = 8
SSD_HEAD_DIM = 64
SSD_INNER = SSD_HEADS * SSD_HEAD_DIM
SSD_GROUPS = 2
SSD_STATE = 128
SSD_CHUNK = 128
D_CONV = 4
CONV_DIM = SSD_INNER + 2 * SSD_GROUPS * SSD_STATE
POOL_WINDOWS = (2, 4, 8, 16)
POOL_GROUP = 128
POOL_DIM = len(POOL_WINDOWS) * POOL_GROUP
ATTN_OUT = ATTN_HEADS * V_DIM
PEER_HEADS = 8
N_KEYS = 128
N_EXPERTS = N_KEYS * N_KEYS
PEER_HALF = 128
PEER_TOPK = 16
HK = PEER_HEADS * PEER_TOPK
HALO = 8
DT_LANE = 64

C_CQ = 0
C_CKV = C_CQ + Q_LORA
C_Z = C_CKV + KV_LORA
C_XBC = C_Z + SSD_INNER
C_POOL = C_XBC + CONV_DIM
C_KD = C_POOL + POOL_DIM
C_KDS = C_KD + 128
C_END = C_KDS + 128

VMEM_LIMIT = 56 * 1024 * 1024

_STAIR = [(a, PEER_TOPK // (a + 1)) for a in range(PEER_TOPK)]
_STAIR_ROWS = sum(n for _, n in _STAIR)
_STAIR_PAD = ((_STAIR_ROWS + 7) // 8) * 8


def _cparams(*sem):
    return pltpu.CompilerParams(dimension_semantics=sem, vmem_limit_bytes=VMEM_LIMIT)


def _const_spec(shape):
    nd = len(shape)
    return pl.BlockSpec(shape, lambda *_: (0,) * nd, pipeline_mode=pl.Buffered(1))


def _rms(x, g):
    return x * lax.rsqrt(jnp.mean(x * x, axis=-1, keepdims=True) + EPS) * g


def _silu(x):
    return x * jax.nn.sigmoid(x)


def _gelu(x):
    return 0.5 * x * (1.0 + lax.erf(x * (1.0 / math.sqrt(2.0))))


def _dot(a, b):
    return jnp.dot(a, b, preferred_element_type=F32)


def _dot_nt(a, b):
    return lax.dot_general(a, b, (((1,), (1,)), ((), ())), preferred_element_type=F32)


def _mod_kernel(c_ref, w_ref, b_ref, o_ref):
    cs = _silu(c_ref[...])
    o_ref[...] = _dot(cs.astype(BF16), w_ref[...].astype(BF16)) + b_ref[...]


def _modulation(c_pad, mod_w, mod_b):
    bn = 1024
    n = mod_w.shape[1]
    return pl.pallas_call(
        _mod_kernel,
        grid=(n // bn,),
        in_specs=[pl.BlockSpec((8, D_MODEL), lambda j: (0, 0)),
                  pl.BlockSpec((D_MODEL, bn), lambda j: (0, j)),
                  pl.BlockSpec((1, bn), lambda j: (0, j))],
        out_specs=pl.BlockSpec((8, bn), lambda j: (0, j)),
        out_shape=jax.ShapeDtypeStruct((8, n), F32),
        compiler_params=_cparams("parallel"),
        name="modulation",
    )(c_pad, mod_w, mod_b)


def _pre_kernel(x_ref, mod_ref, n1g_ref, wcat_ref, qg_ref, wq_ref, wqs_ref, kvg_ref, wk_ref, wv_ref,
                cos_ref, sin_ref, q_ref, k_ref, v_ref, z_ref, xbc_ref, pool_ref, kd_ref):
    x = x_ref[0]
    sh1 = mod_ref[0, 0:1, :]
    sc1 = mod_ref[0, 1:2, :]
    h = _rms(x, n1g_ref[...]) * (1.0 + sc1) + sh1
    proj = _dot(h.astype(BF16), wcat_ref[...])
    z_ref[0] = proj[:, C_Z:C_XBC]
    xbc_ref[0] = proj[:, C_XBC:C_POOL]
    pool_ref[0] = proj[:, C_POOL:C_KD]
    kd = proj[:, C_KD:C_KDS]
    kd_ref[0] = kd
    cos = cos_ref[...]
    sin = sin_ref[...]
    scale = 1.0 / math.sqrt(NOPE_DIM + ROPE_DIM)

    cqn = _rms(proj[:, C_CQ:C_CKV], qg_ref[...]).astype(BF16)
    qm = _dot(cqn, wq_ref[...])
    qs = _dot(cqn, wqs_ref[...])
    for hd in range(ATTN_HEADS):
        o = hd * QK_PAD
        q_ref[0, hd, :, 0:NOPE_DIM] = (qm[:, o:o + NOPE_DIM] * scale).astype(BF16)
        rope = qm[:, o + NOPE_DIM:o + QK_PAD] * cos + qs[:, hd * 128:(hd + 1) * 128] * sin
        q_ref[0, hd, :, NOPE_DIM:QK_PAD] = (rope * scale).astype(BF16)

    ckvn = _rms(proj[:, C_CKV:C_Z], kvg_ref[...]).astype(BF16)
    kn = _dot(ckvn, wk_ref[...])
    vv = _dot(ckvn, wv_ref[...])
    krope = (kd * cos + proj[:, C_KDS:C_END] * sin).astype(BF16)
    for hd in range(ATTN_HEADS):
        k_ref[0, hd, :, 0:NOPE_DIM] = kn[:, hd * NOPE_DIM:(hd + 1) * NOPE_DIM].astype(BF16)
        k_ref[0, hd, :, NOPE_DIM:QK_PAD] = krope
        v_ref[0, hd] = vv[:, hd * V_DIM:(hd + 1) * V_DIM].astype(BF16)


def _pre_mixer(x, mod, w, cos_t, sin_t, tm):
    B, L, _ = x.shape
    grid = (B, L // tm)
    tok = lambda b, i: (b, i, 0)
    head = lambda b, i: (b, 0, i, 0)
    return pl.pallas_call(
        _pre_kernel,
        grid=grid,
        in_specs=[pl.BlockSpec((1, tm, D_MODEL), tok),
                  pl.BlockSpec((1, 6, D_MODEL), lambda b, i: (b, 0, 0)),
                  _const_spec((1, D_MODEL)),
                  _const_spec((D_MODEL, C_END)),
                  _const_spec((1, Q_LORA)),
                  _const_spec((Q_LORA, ATTN_HEADS * QK_PAD)),
                  _const_spec((Q_LORA, ATTN_HEADS * 128)),
                  _const_spec((1, KV_LORA)),
                  _const_spec((KV_LORA, ATTN_HEADS * NOPE_DIM)),
                  _const_spec((KV_LORA, ATTN_HEADS * V_DIM)),
                  pl.BlockSpec((tm, 128), lambda b, i: (i, 0)),
                  pl.BlockSpec((tm, 128), lambda b, i: (i, 0))],
        out_specs=[pl.BlockSpec((1, ATTN_HEADS, tm, QK_PAD), head),
                   pl.BlockSpec((1, ATTN_HEADS, tm, QK_PAD), head),
                   pl.BlockSpec((1, ATTN_HEADS, tm, V_DIM), head),
                   pl.BlockSpec((1, tm, SSD_INNER), tok),
                   pl.BlockSpec((1, tm, CONV_DIM), tok),
                   pl.BlockSpec((1, tm, POOL_DIM), tok),
                   pl.BlockSpec((1, tm, 128), tok)],
        out_shape=[jax.ShapeDtypeStruct((B, ATTN_HEADS, L, QK_PAD), BF16),
                   jax.ShapeDtypeStruct((B, ATTN_HEADS, L, QK_PAD), BF16),
                   jax.ShapeDtypeStruct((B, ATTN_HEADS, L, V_DIM), BF16),
                   jax.ShapeDtypeStruct((B, L, SSD_INNER), F32),
                   jax.ShapeDtypeStruct((B, L, CONV_DIM), F32),
                   jax.ShapeDtypeStruct((B, L, POOL_DIM), F32),
                   jax.ShapeDtypeStruct((B, L, 128), F32)],
        compiler_params=_cparams("parallel", "parallel"),
        name="pre_mixer",
    )(x, mod, w["n1g"], w["wcat"], w["qg"], w["wq"], w["wqs"], w["kvg"], w["wk"], w["wv"], cos_t, sin_t)


def _attn_kernel(q_ref, k_ref, v_ref, o_ref, m_sc, l_sc, acc_sc, *, tk, nk):
    q = q_ref[0, 0]
    m_sc[...] = jnp.full(m_sc.shape, -jnp.inf, F32)
    l_sc[...] = jnp.zeros(l_sc.shape, F32)
    acc_sc[...] = jnp.zeros(acc_sc.shape, F32)

    def body(j, carry):
        off = pl.multiple_of(j * tk, tk)
        kc = k_ref[0, 0, pl.ds(off, tk), :]
        vc = v_ref[0, 0, pl.ds(off, tk), :]
        s = _dot_nt(q, kc)
        m_prev = m_sc[...]
        m_new = jnp.maximum(m_prev, jnp.max(s, axis=1, keepdims=True))
        alpha = jnp.exp(m_prev - m_new)
        p = jnp.exp(s - m_new[:, 0:1])
        l_sc[...] = alpha * l_sc[...] + jnp.sum(p, axis=1, keepdims=True)
        acc_sc[...] = alpha * acc_sc[...] + _dot(p.astype(BF16), vc)
        m_sc[...] = m_new
        return carry

    lax.fori_loop(0, nk, body, 0)
    o_ref[0] = (acc_sc[...] / l_sc[...]).astype(o_ref.dtype)


def _attention(q, k, v, tq, tk):
    B, H, L, _ = q.shape
    kern = functools.partial(_attn_kernel, tk=tk, nk=L // tk)
    return pl.pallas_call(
        kern,
        grid=(B, H, L // tq),
        in_specs=[pl.BlockSpec((1, 1, tq, QK_PAD), lambda b, h, i: (b, h, i, 0)),
                  pl.BlockSpec((1, 1, L, QK_PAD), lambda b, h, i: (b, h, 0, 0)),
                  pl.BlockSpec((1, 1, L, V_DIM), lambda b, h, i: (b, h, 0, 0))],
        out_specs=pl.BlockSpec((1, tq, V_DIM), lambda b, h, i: (b, i, h)),
        out_shape=jax.ShapeDtypeStruct((B, L, H * V_DIM), BF16),
        scratch_shapes=[pltpu.VMEM((tq, 128), F32), pltpu.VMEM((tq, 128), F32), pltpu.VMEM((tq, V_DIM), F32)],
        compiler_params=_cparams("parallel", "parallel", "arbitrary"),
        name="attention",
    )(q, k, v)


def _halo_ext(prev_ref, cur_ref, next_ref, i, nblk):
    prev = jnp.where(i > 0, prev_ref[0], 0.0)
    nxt = jnp.where(i < nblk - 1, next_ref[0], 0.0)
    return jnp.concatenate([prev, cur_ref[0], nxt], axis=0)


def _shift_rows(ext, d, lb):
    n = ext.shape[0]
    r = ext if d == 0 else pltpu.roll(ext, (-d) % n, 0)
    return r[HALO:HALO + lb]


def _ssd_kernel(*refs, reverse, final, nblk, nchunk):
    if final:
        (xp_ref, xc_ref, xn_ref, kd_ref, cw_ref, cb_ref, dtb_ref, a_ref, yf_ref, z_ref, dsk_ref, ng_ref,
         o_ref, st_sc) = refs
    else:
        xp_ref, xc_ref, xn_ref, kd_ref, cw_ref, cb_ref, dtb_ref, a_ref, o_ref, st_sc = refs
    step = pl.program_id(1)
    blk = (nblk - 1 - step) if reverse else step
    lb = nchunk * SSD_CHUNK

    @pl.when(step == 0)
    def _():
        st_sc[...] = jnp.zeros(st_sc.shape, F32)

    ext = _halo_ext(xp_ref, xc_ref, xn_ref, blk, nblk)
    conv = cb_ref[...] + sum(cw_ref[kk:kk + 1, :] * _shift_rows(ext, kk - D_CONV // 2, lb) for kk in range(D_CONV))
    act = _silu(conv)
    dt_all = jax.nn.softplus(kd_ref[0] + dtb_ref[...])
    da_all = dt_all * a_ref[...]

    row = lax.broadcasted_iota(I32, (SSD_CHUNK, SSD_CHUNK), 0)
    col = lax.broadcasted_iota(I32, (SSD_CHUNK, SSD_CHUNK), 1)
    keep = (row <= col) if reverse else (row >= col)
    tri = jnp.where(keep, 1.0, 0.0).astype(F32)
    lane = lax.broadcasted_iota(I32, (SSD_CHUNK, 128), 1)
    low = lane < SSD_HEAD_DIM
    dlane = DT_LANE + (SSD_HEADS if reverse else 0)
    edge = 0 if reverse else SSD_CHUNK - 1

    def pair(arr, h0):
        return jnp.where(low, arr[:, dlane + h0:dlane + h0 + 1], arr[:, dlane + h0 + 1:dlane + h0 + 2])

    ys = []
    for cc in (range(nchunk - 1, -1, -1) if reverse else range(nchunk)):
        r0 = cc * SSD_CHUNK
        da = da_all[r0:r0 + SSD_CHUNK]
        dt = dt_all[r0:r0 + SSD_CHUNK]
        acum = jnp.dot(tri, da, preferred_element_type=F32, precision=lax.Precision.HIGHEST)
        acum_t = acum.T
        a_edge = acum[edge:edge + 1, :]
        ycols = [None] * (SSD_HEADS // 2)
        for g in range(SSD_GROUPS):
            bg = act[r0:r0 + SSD_CHUNK, SSD_INNER + g * SSD_STATE:SSD_INNER + (g + 1) * SSD_STATE]
            cg = act[r0:r0 + SSD_CHUNK, SSD_INNER + (SSD_GROUPS + g) * SSD_STATE:
                     SSD_INNER + (SSD_GROUPS + g + 1) * SSD_STATE]
            cgb = cg.astype(BF16)
            cb = _dot_nt(cgb, bg.astype(BF16))
            bgt = bg.T.astype(BF16)
            for pp in range(SSD_HEADS // SSD_GROUPS // 2):
                pi = g * 2 + pp
                h0 = 2 * pi
                xdt = act[r0:r0 + SSD_CHUNK, pi * 128:(pi + 1) * 128] * pair(dt, h0)
                xdtb = xdt.astype(BF16)
                yd = []
                for hh in (h0, h0 + 1):
                    seg = acum[:, dlane + hh:dlane + hh + 1] - acum_t[dlane + hh:dlane + hh + 1, :]
                    decay = jnp.exp(jnp.where(keep, seg, -jnp.inf))
                    yd.append(_dot((cb * decay).astype(BF16), xdtb))
                y_diag = jnp.where(low, yd[0], yd[1])
                st = st_sc[pi]
                y_off = _dot(cgb, st.astype(BF16)) * jnp.exp(pair(acum, h0))
                ycols[pi] = y_diag + y_off
                to_edge = jnp.exp(pair(a_edge - acum, h0))
                st_sc[pi] = st * jnp.exp(pair(a_edge, h0)) + _dot(bgt, (xdt * to_edge).astype(BF16))
        ys.append((cc, jnp.concatenate(ycols, axis=1)))
    y = jnp.concatenate([v for _, v in sorted(ys, key=lambda t: t[0])], axis=0)

    if final:
        tot = yf_ref[0] + y + act[:, 0:SSD_INNER] * dsk_ref[...]
        o_ref[0] = _rms(tot * _silu(z_ref[0]), ng_ref[...]).astype(o_ref.dtype)
    else:
        o_ref[0] = y


def _ssd_pass(xbc, kd, w, nchunk, reverse, yf=None, z=None):
    B, L, _ = xbc.shape
    lb = nchunk * SSD_CHUNK
    nblk = L // lb
    hb = lb // HALO
    final = yf is not None
    bi = (lambda i: nblk - 1 - i) if reverse else (lambda i: i)
    cur = lambda b, i: (b, bi(i), 0)
    prv = lambda b, i: (b, jnp.maximum(bi(i) * hb - 1, 0), 0)
    nxt = lambda b, i: (b, jnp.minimum((bi(i) + 1) * hb, L // HALO - 1), 0)
    in_specs = [pl.BlockSpec((1, HALO, CONV_DIM), prv),
                pl.BlockSpec((1, lb, CONV_DIM), cur),
                pl.BlockSpec((1, HALO, CONV_DIM), nxt),
                pl.BlockSpec((1, lb, 128), cur),
                _const_spec((D_CONV, CONV_DIM)), _const_spec((1, CONV_DIM)),
                _const_spec((1, 128)), _const_spec((1, 128))]
    args = [xbc, xbc, xbc, kd, w["conv_w"], w["conv_b"], w["dt_bias"], w["a_neg"]]
    if final:
        in_specs += [pl.BlockSpec((1, lb, SSD_INNER), cur), pl.BlockSpec((1, lb, SSD_INNER), cur),
                     _const_spec((1, SSD_INNER)), _const_spec((1, SSD_INNER))]
        args += [yf, z, w["d_skip"], w["ssd_g"]]
    kern = functools.partial(_ssd_kernel, reverse=reverse, final=final, nblk=nblk, nchunk=nchunk)
    return pl.pallas_call(
        kern,
        grid=(B, nblk),
        in_specs=in_specs,
        out_specs=pl.BlockSpec((1, lb, SSD_INNER), cur),
        out_shape=jax.ShapeDtypeStruct((B, L, SSD_INNER), BF16 if final else F32),
        scratch_shapes=[pltpu.VMEM((SSD_HEADS // 2, SSD_STATE, 128), F32)],
        compiler_params=_cparams("parallel", "arbitrary"),
        name="ssd_bwd" if reverse else "ssd_fwd",
    )(*args)


def _pool_kernel(up_ref, uc_ref, un_ref, pw_ref, ps_ref, o_ref, *, nblk, lb, seq):
    blk = pl.program_id(1)
    ext = _halo_ext(up_ref, uc_ref, un_ref, blk, nblk)
    n = ext.shape[0]
    t = blk * lb + lax.broadcasted_iota(I32, (lb, 1), 0)
    for gi, wdw in enumerate(POOL_WINDOWS):
        e = ext[:, gi * POOL_GROUP:(gi + 1) * POOL_GROUP]
        run, width = e, 1
        while width < wdw:
            run = run + pltpu.roll(run, width, 0)
            width *= 2
        tot = _shift_rows(run, wdw // 2 - 1, lb)
        cnt = jnp.minimum(t + wdw // 2, seq) - jnp.maximum(t - wdw // 2, 0)
        d = tot / cnt.astype(F32) - e[HALO:HALO + lb]
        y = _dot(d.astype(BF16), pw_ref[gi])
        o_ref[0, :, gi * POOL_GROUP:(gi + 1) * POOL_GROUP] = (
            y * ps_ref[:, gi * POOL_GROUP:(gi + 1) * POOL_GROUP]).astype(o_ref.dtype)


def _pool(u, w, lb):
    B, L, _ = u.shape
    nblk = L // lb
    hb = lb // HALO
    cur = lambda b, i: (b, i, 0)
    prv = lambda b, i: (b, jnp.maximum(i * hb - 1, 0), 0)
    nxt = lambda b, i: (b, jnp.minimum((i + 1) * hb, L // HALO - 1), 0)
    kern = functools.partial(_pool_kernel, nblk=nblk, lb=lb, seq=L)
    return pl.pallas_call(
        kern,
        grid=(B, nblk),
        in_specs=[pl.BlockSpec((1, HALO, POOL_DIM), prv),
                  pl.BlockSpec((1, lb, POOL_DIM), cur),
                  pl.BlockSpec((1, HALO, POOL_DIM), nxt),
                  _const_spec((len(POOL_WINDOWS), POOL_GROUP, POOL_GROUP)),
                  _const_spec((1, POOL_DIM))],
        out_specs=pl.BlockSpec((1, lb, POOL_DIM), cur),
        out_shape=jax.ShapeDtypeStruct((B, L, POOL_DIM), BF16),
        compiler_params=_cparams("parallel", "parallel"),
        name="pool",
    )(u, u, u, w["pool_w"], w["pool_scale"])


def _post_kernel(x_ref, mod_ref, at_ref, sd_ref, po_ref, woa_ref, wos_ref, wop_ref, n2g_ref, wpq_ref,
                 x1_ref, h2_ref, qp_ref):
    g1 = mod_ref[0, 2:3, :]
    sh2 = mod_ref[0, 3:4, :]
    sc2 = mod_ref[0, 4:5, :]
    mix = _dot(at_ref[0], woa_ref[...]) + _dot(sd_ref[0], wos_ref[...]) + _dot(po_ref[0], wop_ref[...])
    x1 = x_ref[0] + g1 * mix
    x1_ref[0] = x1
    h2 = (_rms(x1, n2g_ref[...]) * (1.0 + sc2) + sh2).astype(BF16)
    h2_ref[0] = h2
    qp_ref[0] = _dot(h2, wpq_ref[...]).astype(BF16)


def _post_mixer(x, mod, attn, ssd, pool, w, tm):
    B, L, _ = x.shape
    tok = lambda b, i: (b, i, 0)
    return pl.pallas_call(
        _post_kernel,
        grid=(B, L // tm),
        in_specs=[pl.BlockSpec((1, tm, D_MODEL), tok),
                  pl.BlockSpec((1, 6, D_MODEL), lambda b, i: (b, 0, 0)),
                  pl.BlockSpec((1, tm, ATTN_OUT), tok),
                  pl.BlockSpec((1, tm, SSD_INNER), tok),
                  pl.BlockSpec((1, tm, POOL_DIM), tok),
                  _const_spec((ATTN_OUT, D_MODEL)), _const_spec((SSD_INNER, D_MODEL)),
                  _const_spec((POOL_DIM, D_MODEL)), _const_spec((1, D_MODEL)),
                  _const_spec((D_MODEL, D_MODEL))],
        out_specs=[pl.BlockSpec((1, tm, D_MODEL), tok)] * 3,
        out_shape=[jax.ShapeDtypeStruct((B, L, D_MODEL), F32),
                   jax.ShapeDtypeStruct((B, L, D_MODEL), BF16),
                   jax.ShapeDtypeStruct((B, L, D_MODEL), BF16)],
        compiler_params=_cparams("parallel", "parallel"),
        name="post_mixer",
    )(x, mod, attn, ssd, pool, w["wo_a"], w["wo_s"], w["wo_p"], w["n2g"], w["wpq"])


def _top16(s, out_v, out_i):
    n = s.shape[0]
    rows = lax.broadcasted_iota(I32, s.shape, 0).astype(F32)
    for r in range(PEER_TOPK):
        m = jnp.max(s, axis=0, keepdims=True)
        idx = jnp.min(jnp.where(s == m, rows, float(n)), axis=0, keepdims=True)
        out_v[r:r + 1, :] = m
        out_i[r:r + 1, :] = idx
        s = jnp.where(rows == idx, -jnp.inf, s)


def _topk_kernel(qp_ref, keys_ref, e_ref, g_ref, v1_sc, i1_sc, v2_sc, i2_sc, cv_sc, ci_sc, tv_sc, ti_sc,
                 eo_sc, go_sc):
    cv_sc[...] = jnp.full(cv_sc.shape, -jnp.inf, F32)
    ci_sc[...] = jnp.zeros(ci_sc.shape, F32)
    for hd in range(PEER_HEADS):
        for half, (vs, is_) in enumerate(((v1_sc, i1_sc), (v2_sc, i2_sc))):
            c0 = (hd * 2 + half) * PEER_HALF
            s = _dot_nt(keys_ref[hd * 2 + half], qp_ref[:, c0:c0 + PEER_HALF])
            _top16(s, vs, is_)
        off = 0
        for a, nb in _STAIR:
            cv_sc[off:off + nb, :] = v1_sc[a:a + 1, :] + v2_sc[0:nb, :]
            ci_sc[off:off + nb, :] = i1_sc[a:a + 1, :] * float(N_KEYS) + i2_sc[0:nb, :]
            off += nb
        cand = cv_sc[...]
        cidx = ci_sc[...]
        rows = lax.broadcasted_iota(I32, cand.shape, 0).astype(F32)
        for r in range(PEER_TOPK):
            m = jnp.max(cand, axis=0, keepdims=True)
            pos = jnp.min(jnp.where(cand == m, rows, float(_STAIR_PAD)), axis=0, keepdims=True)
            hit = rows == pos
            tv_sc[r:r + 1, :] = m
            ti_sc[r:r + 1, :] = jnp.sum(jnp.where(hit, cidx, 0.0), axis=0, keepdims=True)
            cand = jnp.where(hit, -jnp.inf, cand)
        tv = tv_sc[...]
        p = jnp.exp(tv - tv[0:1, :])
        go_sc[hd * PEER_TOPK:(hd + 1) * PEER_TOPK, :] = p / jnp.sum(p, axis=0, keepdims=True)
        eo_sc[hd * PEER_TOPK:(hd + 1) * PEER_TOPK, :] = ti_sc[...]
    e_ref[...] = eo_sc[...].T.astype(I32)
    g_ref[...] = go_sc[...].T


def _peer_topk(qp, keys, tm):
    T = qp.shape[0]
    return pl.pallas_call(
        _topk_kernel,
        grid=(T // tm,),
        in_specs=[pl.BlockSpec((tm, D_MODEL), lambda i: (i, 0)),
                  _const_spec((PEER_HEADS * 2, N_KEYS, PEER_HALF))],
        out_specs=[pl.BlockSpec((tm, HK), lambda i: (i, 0))] * 2,
        out_shape=[jax.ShapeDtypeStruct((T, HK), I32), jax.ShapeDtypeStruct((T, HK), F32)],
        scratch_shapes=[pltpu.VMEM((PEER_TOPK, tm), F32)] * 4
                       + [pltpu.VMEM((_STAIR_PAD, tm), F32)] * 2
                       + [pltpu.VMEM((PEER_TOPK, tm), F32)] * 2
                       + [pltpu.VMEM((HK, tm), F32)] * 2,
        compiler_params=_cparams("parallel"),
        name="peer_topk",
    )(qp, keys)


G_PITCH = N_KEYS + 8


def _gate_kernel(e_ref, g_ref, o_ref, gs_sc, *, tg):
    sub = lax.broadcasted_iota(I32, (N_KEYS, HK), 0)

    def per_token(t, carry):
        er = e_ref[pl.ds(t, 1), :]
        gr = g_ref[pl.ds(t, 1), :]
        i1 = lax.shift_right_logical(er, 7)
        i2 = lax.bitwise_and(er, N_KEYS - 1)
        a = jnp.where(sub == i1, 1.0, 0.0).astype(BF16)
        b = jnp.where(sub == i2, gr, 0.0).astype(BF16)
        gs_sc[pl.ds(pl.multiple_of(t * G_PITCH, 8), N_KEYS), :] = _dot_nt(a, b)
        return carry

    lax.fori_loop(0, tg, per_token, 0)

    def per_key(c, carry):
        col = pl.multiple_of(c * N_KEYS, N_KEYS)
        for t0 in range(0, tg, 16):
            lo = gs_sc[pl.ds(t0 * G_PITCH + c, 8, stride=G_PITCH), :]
            hi = gs_sc[pl.ds((t0 + 8) * G_PITCH + c, 8, stride=G_PITCH), :]
            o_ref[t0:t0 + 16, pl.ds(col, N_KEYS)] = jnp.concatenate([lo, hi], axis=0).astype(o_ref.dtype)
        return carry

    lax.fori_loop(0, N_KEYS, per_key, 0)


def _peer_gates(e, g, tg):
    T = e.shape[0]
    kern = functools.partial(_gate_kernel, tg=tg)
    return pl.pallas_call(
        kern,
        grid=(T // tg,),
        in_specs=[pl.BlockSpec((tg, HK), lambda i: (i, 0))] * 2,
        out_specs=pl.BlockSpec((tg, N_EXPERTS), lambda i: (i, 0)),
        out_shape=jax.ShapeDtypeStruct((T, N_EXPERTS), BF16),
        scratch_shapes=[pltpu.VMEM((tg * G_PITCH, N_KEYS), F32)],
        compiler_params=_cparams("parallel"),
        name="peer_gates",
    )(e, g)


def _dense_kernel(h_ref, u_ref, v_ref, gt_ref, x1_ref, mod_ref, fg_ref, o_ref, acc_sc, *, ne, last):
    j = pl.program_id(2)

    @pl.when(j == 0)
    def _():
        acc_sc[...] = jnp.zeros(acc_sc.shape, F32)

    a = _gelu(_dot_nt(h_ref[0], u_ref[...]))
    wgt = (a * gt_ref[0].astype(F32)).astype(BF16)
    acc_sc[...] += _dot(wgt, v_ref[...])

    @pl.when(j == ne - 1)
    def _():
        x2 = x1_ref[0] + mod_ref[0, 5:6, :] * acc_sc[...]
        o_ref[0] = _rms(x2, fg_ref[...]) if last else x2


def _peer_dense(h2, u, v, gates, x1, mod, fg, tb, eb, last):
    B, L, _ = x1.shape
    ne = N_EXPERTS // eb
    tok = lambda b, i, j: (b, i, 0)
    kern = functools.partial(_dense_kernel, ne=ne, last=last)
    return pl.pallas_call(
        kern,
        grid=(B, L // tb, ne),
        in_specs=[pl.BlockSpec((1, tb, D_MODEL), tok),
                  pl.BlockSpec((eb, D_MODEL), lambda b, i, j: (j, 0)),
                  pl.BlockSpec((eb, D_MODEL), lambda b, i, j: (j, 0)),
                  pl.BlockSpec((1, tb, eb), lambda b, i, j: (b, i, j)),
                  pl.BlockSpec((1, tb, D_MODEL), tok),
                  pl.BlockSpec((1, 6, D_MODEL), lambda b, i, j: (b, 0, 0)),
                  pl.BlockSpec((1, D_MODEL), lambda b, i, j: (0, 0))],
        out_specs=pl.BlockSpec((1, tb, D_MODEL), tok),
        out_shape=jax.ShapeDtypeStruct((B, L, D_MODEL), F32),
        scratch_shapes=[pltpu.VMEM((tb, D_MODEL), F32)],
        compiler_params=_cparams("parallel", "parallel", "arbitrary"),
        name="peer_dense",
    )(h2, u, v, gates, x1, mod, fg)


def _rope_tables(L):
    inv = 1.0 / (ROPE_BASE ** (jnp.arange(0, ROPE_DIM, 2, dtype=F32) / ROPE_DIM))
    ang = jnp.arange(L, dtype=F32)[:, None] * inv[None, :]
    cos, sin = jnp.cos(ang), jnp.sin(ang)
    zero = jnp.zeros((L, 128 - ROPE_DIM), F32)
    return jnp.concatenate([cos, cos, zero], axis=1), jnp.concatenate([-sin, sin, zero], axis=1)


def _swap_halves(w):
    half = w.shape[-1] // 2
    return jnp.concatenate([w[..., half:], w[..., :half]], axis=-1)


def _layer_weights(p, i):
    w_in = p["w_in"][i]
    s0, s1, s2, s3, s4, s5 = (Q_LORA, Q_LORA + KV_LORA, Q_LORA + KV_LORA + ROPE_DIM,
                              Q_LORA + KV_LORA + ROPE_DIM + SSD_INNER,
                              Q_LORA + KV_LORA + ROPE_DIM + SSD_INNER + CONV_DIM,
                              Q_LORA + KV_LORA + ROPE_DIM + SSD_INNER + CONV_DIM + 2 * SSD_HEADS)
    w_kr = w_in[:, s1:s2]
    zcol = lambda n: jnp.zeros((D_MODEL, n), F32)
    wcat = jnp.concatenate(
        [w_in[:, :s0], w_in[:, s0:s1], w_in[:, s2:s3], w_in[:, s3:s4], w_in[:, s5:],
         w_kr, w_in[:, s4:s5], zcol(128 - ROPE_DIM - 2 * SSD_HEADS),
         _swap_halves(w_kr), zcol(128 - ROPE_DIM)], axis=1).astype(BF16)
    wqb = p["w_q_b"][i].reshape(Q_LORA, ATTN_HEADS, NOPE_DIM + ROPE_DIM)
    zq = jnp.zeros((Q_LORA, ATTN_HEADS, 128 - ROPE_DIM), F32)
    wq = jnp.concatenate([wqb, zq], axis=2).reshape(Q_LORA, ATTN_HEADS * QK_PAD).astype(BF16)
    wqs = jnp.concatenate([_swap_halves(wqb[:, :, NOPE_DIM:]), zq], axis=2).reshape(
        Q_LORA, ATTN_HEADS * 128).astype(BF16)
    wkv = p["w_kv_b"][i].reshape(KV_LORA, ATTN_HEADS, NOPE_DIM + V_DIM)
    wk = wkv[:, :, :NOPE_DIM].reshape(KV_LORA, ATTN_HEADS * NOPE_DIM).astype(BF16)
    wv = wkv[:, :, NOPE_DIM:].reshape(KV_LORA, ATTN_HEADS * V_DIM).astype(BF16)
    lane_pad = lambda f, b: jnp.concatenate(
        [jnp.zeros((DT_LANE,), F32), f, b, jnp.zeros((128 - DT_LANE - 2 * SSD_HEADS,), F32)])[None, :]
    w_out = p["w_out"][i].astype(BF16)
    return dict(
        n1g=p["norm1_g"][i][None, :], n2g=p["norm2_g"][i][None, :],
        wcat=wcat, qg=p["q_a_norm_g"][i][None, :], wq=wq, wqs=wqs,
        kvg=p["kv_a_norm_g"][i][None, :], wk=wk, wv=wv,
        conv_w=p["conv_w"][i], conv_b=p["conv_b"][i][None, :],
        dt_bias=lane_pad(p["dt_bias_fwd"][i], p["dt_bias_bwd"][i]),
        a_neg=lane_pad(-jnp.exp(p["a_log_fwd"][i]), -jnp.exp(p["a_log_bwd"][i])),
        d_skip=jnp.repeat(p["d_skip"][i], SSD_HEAD_DIM)[None, :], ssd_g=p["ssd_norm_g"][i][None, :],
        pool_w=p["pool_w"][i].astype(BF16), pool_scale=p["pool_scale"][i][None, :],
        wo_a=w_out[:ATTN_OUT], wo_s=w_out[ATTN_OUT:ATTN_OUT + SSD_INNER], wo_p=w_out[ATTN_OUT + SSD_INNER:],
        wpq=p["peer_wq"][i].astype(BF16),
        keys=p["peer_keys"][i].reshape(PEER_HEADS * 2, N_KEYS, PEER_HALF).astype(BF16),
        u=p["peer_u"][i].astype(BF16), v=p["peer_v"][i].astype(BF16),
    )


def _block(n, pref):
    for c in pref:
        if n % c == 0:
            return c
    raise ValueError(f"no block size in {pref} divides {n}")


def _encoder(x, mods, weights, fg):
    B, L, _ = x.shape
    cos_t, sin_t = _rope_tables(L)
    tm = _block(L, (256, 128))
    tq = _block(L, (512, 256, 128))
    tk = _block(L, (1024, 512, 256, 128))
    nchunk = _block(L // SSD_CHUNK, (4, 2, 1))
    lp = _block(L, (512, 256, 128))
    tb = _block(L, (512, 256, 128))
    tt = _block(B * L, (512, 256, 128))
    tg = _block(B * L, (64, 32, 16))
    for i in range(DEPTH):
        w, mod = weights[i], mods[i]
        q, k, v, z, xbc, pool_in, kd = _pre_mixer(x, mod, w, cos_t, sin_t, tm)
        attn = _attention(q, k, v, tq, tk)
        yf = _ssd_pass(xbc, kd, w, nchunk, reverse=False)
        ssd = _ssd_pass(xbc, kd, w, nchunk, reverse=True, yf=yf, z=z)
        pool = _pool(pool_in, w, lp)
        x1, h2, qp = _post_mixer(x, mod, attn, ssd, pool, w, tm)
        e, g = _peer_topk(qp.reshape(B * L, D_MODEL), w["keys"], tt)
        gates = _peer_gates(e, g, tg).reshape(B, L, N_EXPERTS)
        x = _peer_dense(h2, w["u"], w["v"], gates, x1, mod, fg, tb, 1024, last=(i == DEPTH - 1))
    return x


def kernel(x_prompt, x_sample, c_prompt, c_sample, mod_w, mod_b, norm1_g, norm2_g, w_in, q_a_norm_g, w_q_b, kv_a_norm_g, w_kv_b, conv_w, conv_b, a_log_fwd, a_log_bwd, dt_bias_fwd, dt_bias_bwd, d_skip, ssd_norm_g, pool_w, pool_scale, w_out, peer_wq, peer_keys, peer_u, peer_v, final_norm_g):
    p = dict(mod_w=mod_w, mod_b=mod_b, norm1_g=norm1_g, norm2_g=norm2_g, w_in=w_in,
             q_a_norm_g=q_a_norm_g, w_q_b=w_q_b, kv_a_norm_g=kv_a_norm_g, w_kv_b=w_kv_b,
             conv_w=conv_w, conv_b=conv_b, a_log_fwd=a_log_fwd, a_log_bwd=a_log_bwd,
             dt_bias_fwd=dt_bias_fwd, dt_bias_bwd=dt_bias_bwd, d_skip=d_skip, ssd_norm_g=ssd_norm_g,
             pool_w=pool_w, pool_scale=pool_scale, w_out=w_out, peer_wq=peer_wq, peer_keys=peer_keys,
             peer_u=peer_u, peer_v=peer_v)
    weights = [_layer_weights(p, i) for i in range(DEPTH)]
    bp, bs = c_prompt.shape[0], c_sample.shape[0]
    c_pad = jnp.concatenate([c_prompt, c_sample, jnp.zeros((8 - bp - bs, D_MODEL), F32)], axis=0)
    mods = [_modulation(c_pad, mod_w[i], mod_b[i][None, :]) for i in range(DEPTH)]
    fg = final_norm_g[None, :]
    mods_p = [m[:bp].reshape(bp, 6, D_MODEL) for m in mods]
    mods_s = [m[bp:bp + bs].reshape(bs, 6, D_MODEL) for m in mods]
    return (_encoder(x_prompt, mods_p, weights, fg), _encoder(x_sample, mods_s, weights, fg))
```

```python
import functools
import math

import jax
import jax.numpy as jnp
from jax import lax
from jax.experimental import pallas as pl
from jax.experimental.pallas import tpu as pltpu

F32 = jnp.float32
BF16 = jnp.bfloat16
I32 = jnp.int32

D_MODEL = 2048
DEPTH = 2
EPS = 1e-6
ATTN_HEADS = 8
Q_LORA = 512
KV_LORA = 256
NOPE_DIM = 128
ROPE_DIM = 64
V_DIM = 128
ROPE_BASE = 10000.0
QK_PAD = 256
SSD_HEADS = 8
SSD_HEAD_DIM = 64
SSD_INNER = SSD_HEADS * SSD_HEAD_DIM
SSD_GROUPS = 2
SSD_STATE = 128
SSD_CHUNK = 128
D_CONV = 4
CONV_DIM = SSD_INNER + 2 * SSD_GROUPS * SSD_STATE
POOL_WINDOWS = (2, 4, 8, 16)
POOL_GROUP = 128
POOL_DIM = len(POOL_WINDOWS) * POOL_GROUP
ATTN_OUT = ATTN_HEADS * V_DIM
PEER_HEADS = 8
N_KEYS = 128
N_EXPERTS = N_KEYS * N_KEYS
PEER_HALF = 128
PEER_TOPK = 16
HK = PEER_HEADS * PEER_TOPK
HALO = 8
DT_LANE = 64

C_CQ = 0
C_CKV = C_CQ + Q_LORA
C_Z = C_CKV + KV_LORA
C_XBC = C_Z + SSD_INNER
C_POOL = C_XBC + CONV_DIM
C_KD = C_POOL + POOL_DIM
C_KDS = C_KD + 128
C_END = C_KDS + 128

VMEM_LIMIT = 56 * 1024 * 1024

_STAIR = [(a, PEER_TOPK // (a + 1)) for a in range(PEER_TOPK)]
_STAIR_ROWS = sum(n for _, n in _STAIR)
_STAIR_PAD = ((_STAIR_ROWS + 7) // 8) * 8


def _cparams(*sem):
    return pltpu.CompilerParams(dimension_semantics=sem, vmem_limit_bytes=VMEM_LIMIT)


def _const_spec(shape):
    nd = len(shape)
    return pl.BlockSpec(shape, lambda *_: (0,) * nd, pipeline_mode=pl.Buffered(1))


def _rms(x, g):
    return x * lax.rsqrt(jnp.mean(x * x, axis=-1, keepdims=True) + EPS) * g


def _silu(x):
    return x * jax.nn.sigmoid(x)


def _gelu(x):
    return 0.5 * x * (1.0 + lax.erf(x * (1.0 / math.sqrt(2.0))))


def _dot(a, b):
    return jnp.dot(a, b, preferred_element_type=F32)


def _dot_nt(a, b):
    return lax.dot_general(a, b, (((1,), (1,)), ((), ())), preferred_element_type=F32)


def _mod_kernel(c_ref, w_ref, b_ref, o_ref):
    cs = _silu(c_ref[...])
    o_ref[...] = _dot(cs.astype(BF16), w_ref[...].astype(BF16)) + b_ref[...]


def _modulation(c_pad, mod_w, mod_b):
    bn = 1024
    n = mod_w.shape[1]
    return pl.pallas_call(
        _mod_kernel,
        grid=(n // bn,),
        in_specs=[pl.BlockSpec((8, D_MODEL), lambda j: (0, 0)),
                  pl.BlockSpec((D_MODEL, bn), lambda j: (0, j)),
                  pl.BlockSpec((1, bn), lambda j: (0, j))],
        out_specs=pl.BlockSpec((8, bn), lambda j: (0, j)),
        out_shape=jax.ShapeDtypeStruct((8, n), F32),
        compiler_params=_cparams("parallel"),
        name="modulation",
    )(c_pad, mod_w, mod_b)


def _pre_kernel(x_ref, mod_ref, n1g_ref, wcat_ref, qg_ref, wq_ref, wqs_ref, kvg_ref, wk_ref, wv_ref,
                cos_ref, sin_ref, q_ref, k_ref, v_ref, z_ref, xbc_ref, pool_ref, kd_ref):
    x = x_ref[0]
    sh1 = mod_ref[0, 0:1, :]
    sc1 = mod_ref[0, 1:2, :]
    h = _rms(x, n1g_ref[...]) * (1.0 + sc1) + sh1
    proj = _dot(h.astype(BF16), wcat_ref[...])
    z_ref[0] = proj[:, C_Z:C_XBC]
    xbc_ref[0] = proj[:, C_XBC:C_POOL]
    pool_ref[0] = proj[:, C_POOL:C_KD]
    kd = proj[:, C_KD:C_KDS]
    kd_ref[0] = kd
    cos = cos_ref[...]
    sin = sin_ref[...]
    scale = math.log2(math.e) / math.sqrt(NOPE_DIM + ROPE_DIM)

    cqn = _rms(proj[:, C_CQ:C_CKV], qg_ref[...]).astype(BF16)
    qm = _dot(cqn, wq_ref[...])
    qs = _dot(cqn, wqs_ref[...])
    for hd in range(ATTN_HEADS):
        o = hd * QK_PAD
        q_ref[0, hd, :, 0:NOPE_DIM] = (qm[:, o:o + NOPE_DIM] * scale).astype(BF16)
        rope = qm[:, o + NOPE_DIM:o + QK_PAD] * cos + qs[:, hd * 128:(hd + 1) * 128] * sin
        q_ref[0, hd, :, NOPE_DIM:QK_PAD] = (rope * scale).astype(BF16)

    ckvn = _rms(proj[:, C_CKV:C_Z], kvg_ref[...]).astype(BF16)
    kn = _dot(ckvn, wk_ref[...])
    vvt = _dot_nt(wv_ref[...], ckvn)
    krope = (kd * cos + proj[:, C_KDS:C_END] * sin).astype(BF16)
    for hd in range(ATTN_HEADS):
        k_ref[0, hd, :, 0:NOPE_DIM] = kn[:, hd * NOPE_DIM:(hd + 1) * NOPE_DIM].astype(BF16)
        k_ref[0, hd, :, NOPE_DIM:QK_PAD] = krope
        v_ref[0, hd] = vvt[hd * V_DIM:(hd + 1) * V_DIM, :].astype(BF16)


def _pre_mixer(x, mod, w, cos_t, sin_t, tm):
    B, L, _ = x.shape
    grid = (B, L // tm)
    tok = lambda b, i: (b, i, 0)
    head = lambda b, i: (b, 0, i, 0)
    return pl.pallas_call(
        _pre_kernel,
        grid=grid,
        in_specs=[pl.BlockSpec((1, tm, D_MODEL), tok),
                  pl.BlockSpec((1, 6, D_MODEL), lambda b, i: (b, 0, 0)),
                  _const_spec((1, D_MODEL)),
                  _const_spec((D_MODEL, C_END)),
                  _const_spec((1, Q_LORA)),
                  _const_spec((Q_LORA, ATTN_HEADS * QK_PAD)),
                  _const_spec((Q_LORA, ATTN_HEADS * 128)),
                  _const_spec((1, KV_LORA)),
                  _const_spec((KV_LORA, ATTN_HEADS * NOPE_DIM)),
                  _const_spec((ATTN_HEADS * V_DIM, KV_LORA)),
                  pl.BlockSpec((tm, 128), lambda b, i: (i, 0)),
                  pl.BlockSpec((tm, 128), lambda b, i: (i, 0))],
        out_specs=[pl.BlockSpec((1, ATTN_HEADS, tm, QK_PAD), head),
                   pl.BlockSpec((1, ATTN_HEADS, tm, QK_PAD), head),
                   pl.BlockSpec((1, ATTN_HEADS, V_DIM, tm), lambda b, i: (b, 0, 0, i)),
                   pl.BlockSpec((1, tm, SSD_INNER), tok),
                   pl.BlockSpec((1, tm, CONV_DIM), tok),
                   pl.BlockSpec((1, tm, POOL_DIM), tok),
                   pl.BlockSpec((1, tm, 128), tok)],
        out_shape=[jax.ShapeDtypeStruct((B, ATTN_HEADS, L, QK_PAD), BF16),
                   jax.ShapeDtypeStruct((B, ATTN_HEADS, L, QK_PAD), BF16),
                   jax.ShapeDtypeStruct((B, ATTN_HEADS, V_DIM, L), BF16),
                   jax.ShapeDtypeStruct((B, L, SSD_INNER), F32),
                   jax.ShapeDtypeStruct((B, L, CONV_DIM), F32),
                   jax.ShapeDtypeStruct((B, L, POOL_DIM), F32),
                   jax.ShapeDtypeStruct((B, L, 128), F32)],
        compiler_params=_cparams("parallel", "parallel"),
        name="pre_mixer",
    )(x, mod, w["n1g"], w["wcat"], w["qg"], w["wq"], w["wqs"], w["kvg"], w["wk"], w["wv"], cos_t, sin_t)


def _attn_kernel(q_ref, k_ref, vt_ref, o_ref, st_sc, p_sc, *, tk, nk):
    q = q_ref[0, 0]
    tq = q.shape[0]

    def scores(j, slot):
        off = pl.multiple_of(j * tk, tk)
        st_sc[slot] = _dot_nt(k_ref[0, 0, pl.ds(off, tk), :], q)

    def weighted_values(j, slot):
        off = pl.multiple_of(j * tk, tk)
        return _dot(vt_ref[0, 0, :, pl.ds(off, tk)], p_sc[slot])

    def step(j, cur, carry):
        m_prev, l_prev, acc, a_prev = carry
        nxt = 1 - cur
        acc_new = a_prev * acc + weighted_values(jnp.maximum(j - 1, 0), nxt)
        scores(jnp.minimum(j + 1, nk - 1), nxt)
        m_new = jnp.maximum(m_prev, jnp.max(st_sc[cur], axis=0, keepdims=True))
        alpha = jnp.exp2(m_prev - m_new)
        p = jnp.exp2(st_sc[cur] - m_new)
        l_new = alpha * l_prev + jnp.sum(p, axis=0, keepdims=True)
        p_sc[cur] = p.astype(BF16)
        return m_new, l_new, acc_new, alpha

    def body(jj, carry):
        return step(2 * jj + 1, 1, step(2 * jj, 0, carry))

    scores(0, 0)
    p_sc[1] = jnp.zeros((tk, tq), BF16)
    init = (jnp.full((1, tq), -jnp.inf, F32), jnp.zeros((1, tq), F32), jnp.zeros((V_DIM, tq), F32),
            jnp.ones((1, tq), F32))
    _, l, acc, a_last = lax.fori_loop(0, nk // 2, body, init)
    acc = a_last * acc + weighted_values(nk - 1, 1)
    o_ref[0] = (acc / l).T.astype(o_ref.dtype)


def _attention(q, k, vt, tq, tk):
    B, H, L, _ = q.shape
    nk = L // tk
    assert nk % 2 == 0, (L, tk)
    kern = functools.partial(_attn_kernel, tk=tk, nk=nk)
    return pl.pallas_call(
        kern,
        grid=(B, H, L // tq),
        in_specs=[pl.BlockSpec((1, 1, tq, QK_PAD), lambda b, h, i: (b, h, i, 0)),
                  pl.BlockSpec((1, 1, L, QK_PAD), lambda b, h, i: (b, h, 0, 0)),
                  pl.BlockSpec((1, 1, V_DIM, L), lambda b, h, i: (b, h, 0, 0))],
        out_specs=pl.BlockSpec((1, tq, V_DIM), lambda b, h, i: (b, i, h)),
        out_shape=jax.ShapeDtypeStruct((B, L, H * V_DIM), BF16),
        scratch_shapes=[pltpu.VMEM((2, tk, tq), F32), pltpu.VMEM((2, tk, tq), BF16)],
        compiler_params=_cparams("parallel", "parallel", "arbitrary"),
        name="attention",
    )(q, k, vt)


def _halo_ext(prev_ref, cur_ref, next_ref, i, nblk):
    prev = jnp.where(i > 0, prev_ref[0], 0.0)
    nxt = jnp.where(i < nblk - 1, next_ref[0], 0.0)
    return jnp.concatenate([prev, cur_ref[0], nxt], axis=0)


def _shift_rows(ext, d, lb):
    n = ext.shape[0]
    r = ext if d == 0 else pltpu.roll(ext, (-d) % n, 0)
    return r[HALO:HALO + lb]


def _ssd_kernel(*refs, reverse, final, nblk, nchunk):
    if final:
        (xp_ref, xc_ref, xn_ref, kd_ref, cw_ref, cb_ref, dtb_ref, a_ref, yf_ref, z_ref, dsk_ref, ng_ref,
         o_ref, st_sc) = refs
    else:
        xp_ref, xc_ref, xn_ref, kd_ref, cw_ref, cb_ref, dtb_ref, a_ref, o_ref, st_sc = refs
    step = pl.program_id(1)
    blk = (nblk - 1 - step) if reverse else step
    lb = nchunk * SSD_CHUNK

    @pl.when(step == 0)
    def _():
        st_sc[...] = jnp.zeros(st_sc.shape, F32)

    ext = _halo_ext(xp_ref, xc_ref, xn_ref, blk, nblk)
    conv = cb_ref[...] + sum(cw_ref[kk:kk + 1, :] * _shift_rows(ext, kk - D_CONV // 2, lb) for kk in range(D_CONV))
    act = _silu(conv)
    dt_all = jax.nn.softplus(kd_ref[0] + dtb_ref[...])
    da_all = dt_all * a_ref[...]

    row = lax.broadcasted_iota(I32, (SSD_CHUNK, SSD_CHUNK), 0)
    col = lax.broadcasted_iota(I32, (SSD_CHUNK, SSD_CHUNK), 1)
    keep = (row <= col) if reverse else (row >= col)
    tri = jnp.where(keep, 1.0, 0.0).astype(F32)
    lane = lax.broadcasted_iota(I32, (SSD_CHUNK, 128), 1)
    low = lane < SSD_HEAD_DIM
    dlane = DT_LANE + (SSD_HEADS if reverse else 0)
    edge = 0 if reverse else SSD_CHUNK - 1

    def pair(arr, h0):
        return jnp.where(low, arr[:, dlane + h0:dlane + h0 + 1], arr[:, dlane + h0 + 1:dlane + h0 + 2])

    ys = []
    for cc in (range(nchunk - 1, -1, -1) if reverse else range(nchunk)):
        r0 = cc * SSD_CHUNK
        da = da_all[r0:r0 + SSD_CHUNK]
        dt = dt_all[r0:r0 + SSD_CHUNK]
        acum = jnp.dot(tri, da, preferred_element_type=F32, precision=lax.Precision.HIGHEST)
        acum_t = acum.T
        a_edge = acum[edge:edge + 1, :]
        ycols = [None] * (SSD_HEADS // 2)
        for g in range(SSD_GROUPS):
            bg = act[r0:r0 + SSD_CHUNK, SSD_INNER + g * SSD_STATE:SSD_INNER + (g + 1) * SSD_STATE]
            cg = act[r0:r0 + SSD_CHUNK, SSD_INNER + (SSD_GROUPS + g) * SSD_STATE:
                     SSD_INNER + (SSD_GROUPS + g + 1) * SSD_STATE]
            cgb = cg.astype(BF16)
            cb = _dot_nt(cgb, bg.astype(BF16))
            bgt = bg.T.astype(BF16)
            for pp in range(SSD_HEADS // SSD_GROUPS // 2):
                pi = g * 2 + pp
                h0 = 2 * pi
                xdt = act[r0:r0 + SSD_CHUNK, pi * 128:(pi + 1) * 128] * pair(dt, h0)
                xdtb = xdt.astype(BF16)
                yd = []
                for hh in (h0, h0 + 1):
                    seg = acum[:, dlane + hh:dlane + hh + 1] - acum_t[dlane + hh:dlane + hh + 1, :]
                    decay = jnp.exp(jnp.where(keep, seg, -jnp.inf))
                    yd.append(_dot((cb * decay).astype(BF16), xdtb))
                y_diag = jnp.where(low, yd[0], yd[1])
                st = st_sc[pi]
                y_off = _dot(cgb, st.astype(BF16)) * jnp.exp(pair(acum, h0))
                ycols[pi] = y_diag + y_off
                to_edge = jnp.exp(pair(a_edge - acum, h0))
                st_sc[pi] = st * jnp.exp(pair(a_edge, h0)) + _dot(bgt, (xdt * to_edge).astype(BF16))
        ys.append((cc, jnp.concatenate(ycols, axis=1)))
    y = jnp.concatenate([v for _, v in sorted(ys, key=lambda t: t[0])], axis=0)

    if final:
        tot = yf_ref[0] + y + act[:, 0:SSD_INNER] * dsk_ref[...]
        o_ref[0] = _rms(tot * _silu(z_ref[0]), ng_ref[...]).astype(o_ref.dtype)
    else:
        o_ref[0] = y


def _ssd_pass(xbc, kd, w, nchunk, reverse, yf=None, z=None):
    B, L, _ = xbc.shape
    lb = nchunk * SSD_CHUNK
    nblk = L // lb
    hb = lb // HALO
    final = yf is not None
    bi = (lambda i: nblk - 1 - i) if reverse else (lambda i: i)
    cur = lambda b, i: (b, bi(i), 0)
    prv = lambda b, i: (b, jnp.maximum(bi(i) * hb - 1, 0), 0)
    nxt = lambda b, i: (b, jnp.minimum((bi(i) + 1) * hb, L // HALO - 1), 0)
    in_specs = [pl.BlockSpec((1, HALO, CONV_DIM), prv),
                pl.BlockSpec((1, lb, CONV_DIM), cur),
                pl.BlockSpec((1, HALO, CONV_DIM), nxt),
                pl.BlockSpec((1, lb, 128), cur),
                _const_spec((D_CONV, CONV_DIM)), _const_spec((1, CONV_DIM)),
                _const_spec((1, 128)), _const_spec((1, 128))]
    args = [xbc, xbc, xbc, kd, w["conv_w"], w["conv_b"], w["dt_bias"], w["a_neg"]]
    if final:
        in_specs += [pl.BlockSpec((1, lb, SSD_INNER), cur), pl.BlockSpec((1, lb, SSD_INNER), cur),
                     _const_spec((1, SSD_INNER)), _const_spec((1, SSD_INNER))]
        args += [yf, z, w["d_skip"], w["ssd_g"]]
    kern = functools.partial(_ssd_kernel, reverse=reverse, final=final, nblk=nblk, nchunk=nchunk)
    return pl.pallas_call(
        kern,
        grid=(B, nblk),
        in_specs=in_specs,
        out_specs=pl.BlockSpec((1, lb, SSD_INNER), cur),
        out_shape=jax.ShapeDtypeStruct((B, L, SSD_INNER), BF16 if final else F32),
        scratch_shapes=[pltpu.VMEM((SSD_HEADS // 2, SSD_STATE, 128), F32)],
        compiler_params=_cparams("parallel", "arbitrary"),
        name="ssd_bwd" if reverse else "ssd_fwd",
    )(*args)


def _pool_kernel(up_ref, uc_ref, un_ref, pw_ref, ps_ref, o_ref, *, nblk, lb, seq):
    blk = pl.program_id(1)
    ext = _halo_ext(up_ref, uc_ref, un_ref, blk, nblk)
    n = ext.shape[0]
    t = blk * lb + lax.broadcasted_iota(I32, (lb, 1), 0)
    for gi, wdw in enumerate(POOL_WINDOWS):
        e = ext[:, gi * POOL_GROUP:(gi + 1) * POOL_GROUP]
        run, width = e, 1
        while width < wdw:
            run = run + pltpu.roll(run, width, 0)
            width *= 2
        tot = _shift_rows(run, wdw // 2 - 1, lb)
        cnt = jnp.minimum(t + wdw // 2, seq) - jnp.maximum(t - wdw // 2, 0)
        d = tot / cnt.astype(F32) - e[HALO:HALO + lb]
        y = _dot(d.astype(BF16), pw_ref[gi])
        o_ref[0, :, gi * POOL_GROUP:(gi + 1) * POOL_GROUP] = (
            y * ps_ref[:, gi * POOL_GROUP:(gi + 1) * POOL_GROUP]).astype(o_ref.dtype)


def _pool(u, w, lb):
    B, L, _ = u.shape
    nblk = L // lb
    hb = lb // HALO
    cur = lambda b, i: (b, i, 0)
    prv = lambda b, i: (b, jnp.maximum(i * hb - 1, 0), 0)
    nxt = lambda b, i: (b, jnp.minimum((i + 1) * hb, L // HALO - 1), 0)
    kern = functools.partial(_pool_kernel, nblk=nblk, lb=lb, seq=L)
    return pl.pallas_call(
        kern,
        grid=(B, nblk),
        in_specs=[pl.BlockSpec((1, HALO, POOL_DIM), prv),
                  pl.BlockSpec((1, lb, POOL_DIM), cur),
                  pl.BlockSpec((1, HALO, POOL_DIM), nxt),
                  _const_spec((len(POOL_WINDOWS), POOL_GROUP, POOL_GROUP)),
                  _const_spec((1, POOL_DIM))],
        out_specs=pl.BlockSpec((1, lb, POOL_DIM), cur),
        out_shape=jax.ShapeDtypeStruct((B, L, POOL_DIM), BF16),
        compiler_params=_cparams("parallel", "parallel"),
        name="pool",
    )(u, u, u, w["pool_w"], w["pool_scale"])


def _post_kernel(x_ref, mod_ref, at_ref, sd_ref, po_ref, woa_ref, wos_ref, wop_ref, n2g_ref, wpq_ref,
                 x1_ref, h2_ref, qp_ref):
    g1 = mod_ref[0, 2:3, :]
    sh2 = mod_ref[0, 3:4, :]
    sc2 = mod_ref[0, 4:5, :]
    mix = _dot(at_ref[0], woa_ref[...]) + _dot(sd_ref[0], wos_ref[...]) + _dot(po_ref[0], wop_ref[...])
    x1 = x_ref[0] + g1 * mix
    x1_ref[0] = x1
    h2 = (_rms(x1, n2g_ref[...]) * (1.0 + sc2) + sh2).astype(BF16)
    h2_ref[0] = h2
    qp_ref[0] = _dot(h2, wpq_ref[...]).astype(BF16)


def _post_mixer(x, mod, attn, ssd, pool, w, tm):
    B, L, _ = x.shape
    tok = lambda b, i: (b, i, 0)
    return pl.pallas_call(
        _post_kernel,
        grid=(B, L // tm),
        in_specs=[pl.BlockSpec((1, tm, D_MODEL), tok),
                  pl.BlockSpec((1, 6, D_MODEL), lambda b, i: (b, 0, 0)),
                  pl.BlockSpec((1, tm, ATTN_OUT), tok),
                  pl.BlockSpec((1, tm, SSD_INNER), tok),
                  pl.BlockSpec((1, tm, POOL_DIM), tok),
                  _const_spec((ATTN_OUT, D_MODEL)), _const_spec((SSD_INNER, D_MODEL)),
                  _const_spec((POOL_DIM, D_MODEL)), _const_spec((1, D_MODEL)),
                  _const_spec((D_MODEL, D_MODEL))],
        out_specs=[pl.BlockSpec((1, tm, D_MODEL), tok)] * 3,
        out_shape=[jax.ShapeDtypeStruct((B, L, D_MODEL), F32),
                   jax.ShapeDtypeStruct((B, L, D_MODEL), BF16),
                   jax.ShapeDtypeStruct((B, L, D_MODEL), BF16)],
        compiler_params=_cparams("parallel", "parallel"),
        name="post_mixer",
    )(x, mod, attn, ssd, pool, w["wo_a"], w["wo_s"], w["wo_p"], w["n2g"], w["wpq"])


def _top16(s, out_v, out_i):
    n = s.shape[0]
    rows = lax.broadcasted_iota(I32, s.shape, 0).astype(F32)
    for r in range(PEER_TOPK):
        m = jnp.max(s, axis=0, keepdims=True)
        idx = jnp.min(jnp.where(s == m, rows, float(n)), axis=0, keepdims=True)
        out_v[r:r + 1, :] = m
        out_i[r:r + 1, :] = idx
        s = jnp.where(rows == idx, -jnp.inf, s)


def _topk_kernel(qp_ref, keys_ref, e_ref, g_ref, v1_sc, i1_sc, v2_sc, i2_sc, cv_sc, ci_sc, tv_sc, ti_sc,
                 eo_sc, go_sc):
    cv_sc[...] = jnp.full(cv_sc.shape, -jnp.inf, F32)
    ci_sc[...] = jnp.zeros(ci_sc.shape, F32)
    for hd in range(PEER_HEADS):
        for half, (vs, is_) in enumerate(((v1_sc, i1_sc), (v2_sc, i2_sc))):
            c0 = (hd * 2 + half) * PEER_HALF
            s = _dot_nt(keys_ref[hd * 2 + half], qp_ref[:, c0:c0 + PEER_HALF])
            _top16(s, vs, is_)
        off = 0
        for a, nb in _STAIR:
            cv_sc[off:off + nb, :] = v1_sc[a:a + 1, :] + v2_sc[0:nb, :]
            ci_sc[off:off + nb, :] = i1_sc[a:a + 1, :] * float(N_KEYS) + i2_sc[0:nb, :]
            off += nb
        cand = cv_sc[...]
        cidx = ci_sc[...]
        rows = lax.broadcasted_iota(I32, cand.shape, 0).astype(F32)
        for r in range(PEER_TOPK):
            m = jnp.max(cand, axis=0, keepdims=True)
            pos = jnp.min(jnp.where(cand == m, rows, float(_STAIR_PAD)), axis=0, keepdims=True)
            hit = rows == pos
            tv_sc[r:r + 1, :] = m
            ti_sc[r:r + 1, :] = jnp.sum(jnp.where(hit, cidx, 0.0), axis=0, keepdims=True)
            cand = jnp.where(hit, -jnp.inf, cand)
        tv = tv_sc[...]
        p = jnp.exp(tv - tv[0:1, :])
        go_sc[hd * PEER_TOPK:(hd + 1) * PEER_TOPK, :] = p / jnp.sum(p, axis=0, keepdims=True)
        eo_sc[hd * PEER_TOPK:(hd + 1) * PEER_TOPK, :] = ti_sc[...]
    e_ref[...] = eo_sc[...].T.astype(I32)
    g_ref[...] = go_sc[...].T


def _peer_topk(qp, keys, tm):
    T = qp.shape[0]
    return pl.pallas_call(
        _topk_kernel,
        grid=(T // tm,),
        in_specs=[pl.BlockSpec((tm, D_MODEL), lambda i: (i, 0)),
                  _const_spec((PEER_HEADS * 2, N_KEYS, PEER_HALF))],
        out_specs=[pl.BlockSpec((tm, HK), lambda i: (i, 0))] * 2,
        out_shape=[jax.ShapeDtypeStruct((T, HK), I32), jax.ShapeDtypeStruct((T, HK), F32)],
        scratch_shapes=[pltpu.VMEM((PEER_TOPK, tm), F32)] * 4
                       + [pltpu.VMEM((_STAIR_PAD, tm), F32)] * 2
                       + [pltpu.VMEM((PEER_TOPK, tm), F32)] * 2
                       + [pltpu.VMEM((HK, tm), F32)] * 2,
        compiler_params=_cparams("parallel"),
        name="peer_topk",
    )(qp, keys)


G_PITCH = N_KEYS + 8


def _gate_kernel(e_ref, g_ref, o_ref, gs_sc, *, tg):
    sub = lax.broadcasted_iota(I32, (N_KEYS, HK), 0)

    def per_token(t, carry):
        er = e_ref[pl.ds(t, 1), :]
        gr = g_ref[pl.ds(t, 1), :]
        i1 = lax.shift_right_logical(er, 7)
        i2 = lax.bitwise_and(er, N_KEYS - 1)
        a = jnp.where(sub == i1, 1.0, 0.0).astype(BF16)
        b = jnp.where(sub == i2, gr, 0.0).astype(BF16)
        gs_sc[pl.ds(pl.multiple_of(t * G_PITCH, 8), N_KEYS), :] = _dot_nt(a, b)
        return carry

    lax.fori_loop(0, tg, per_token, 0, unroll=16)

    def per_key(c, carry):
        col = pl.multiple_of(c * N_KEYS, N_KEYS)
        for t0 in range(0, tg, 16):
            lo = gs_sc[pl.ds(t0 * G_PITCH + c, 8, stride=G_PITCH), :]
            hi = gs_sc[pl.ds((t0 + 8) * G_PITCH + c, 8, stride=G_PITCH), :]
            o_ref[t0:t0 + 16, pl.ds(col, N_KEYS)] = jnp.concatenate([lo, hi], axis=0).astype(o_ref.dtype)
        return carry

    lax.fori_loop(0, N_KEYS, per_key, 0, unroll=2)


def _peer_gates(e, g, tg):
    T = e.shape[0]
    kern = functools.partial(_gate_kernel, tg=tg)
    return pl.pallas_call(
        kern,
        grid=(T // tg,),
        in_specs=[pl.BlockSpec((tg, HK), lambda i: (i, 0))] * 2,
        out_specs=pl.BlockSpec((tg, N_EXPERTS), lambda i: (i, 0)),
        out_shape=jax.ShapeDtypeStruct((T, N_EXPERTS), BF16),
        scratch_shapes=[pltpu.VMEM((tg * G_PITCH, N_KEYS), F32)],
        compiler_params=_cparams("parallel"),
        name="peer_gates",
    )(e, g)


def _dense_kernel(h_ref, u_ref, v_ref, gt_ref, x1_ref, mod_ref, fg_ref, o_ref, acc_sc, *, ne, last):
    j = pl.program_id(2)

    @pl.when(j == 0)
    def _():
        acc_sc[...] = jnp.zeros(acc_sc.shape, F32)

    a = _gelu(_dot_nt(h_ref[0], u_ref[...]))
    wgt = (a * gt_ref[0].astype(F32)).astype(BF16)
    acc_sc[...] += _dot(wgt, v_ref[...])

    @pl.when(j == ne - 1)
    def _():
        x2 = x1_ref[0] + mod_ref[0, 5:6, :] * acc_sc[...]
        o_ref[0] = _rms(x2, fg_ref[...]) if last else x2


def _peer_dense(h2, u, v, gates, x1, mod, fg, tb, eb, last):
    B, L, _ = x1.shape
    ne = N_EXPERTS // eb
    tok = lambda b, i, j: (b, i, 0)
    kern = functools.partial(_dense_kernel, ne=ne, last=last)
    return pl.pallas_call(
        kern,
        grid=(B, L // tb, ne),
        in_specs=[pl.BlockSpec((1, tb, D_MODEL), tok),
                  pl.BlockSpec((eb, D_MODEL), lambda b, i, j: (j, 0)),
                  pl.BlockSpec((eb, D_MODEL), lambda b, i, j: (j, 0)),
                  pl.BlockSpec((1, tb, eb), lambda b, i, j: (b, i, j)),
                  pl.BlockSpec((1, tb, D_MODEL), tok),
                  pl.BlockSpec((1, 6, D_MODEL), lambda b, i, j: (b, 0, 0)),
                  pl.BlockSpec((1, D_MODEL), lambda b, i, j: (0, 0))],
        out_specs=pl.BlockSpec((1, tb, D_MODEL), tok),
        out_shape=jax.ShapeDtypeStruct((B, L, D_MODEL), F32),
        scratch_shapes=[pltpu.VMEM((tb, D_MODEL), F32)],
        compiler_params=_cparams("parallel", "parallel", "arbitrary"),
        name="peer_dense",
    )(h2, u, v, gates, x1, mod, fg)


def _rope_tables(L):
    inv = 1.0 / (ROPE_BASE ** (jnp.arange(0, ROPE_DIM, 2, dtype=F32) / ROPE_DIM))
    ang = jnp.arange(L, dtype=F32)[:, None] * inv[None, :]
    cos, sin = jnp.cos(ang), jnp.sin(ang)
    zero = jnp.zeros((L, 128 - ROPE_DIM), F32)
    return jnp.concatenate([cos, cos, zero], axis=1), jnp.concatenate([-sin, sin, zero], axis=1)


def _swap_halves(w):
    half = w.shape[-1] // 2
    return jnp.concatenate([w[..., half:], w[..., :half]], axis=-1)


def _layer_weights(p, i):
    w_in = p["w_in"][i]
    s0, s1, s2, s3, s4, s5 = (Q_LORA, Q_LORA + KV_LORA, Q_LORA + KV_LORA + ROPE_DIM,
                              Q_LORA + KV_LORA + ROPE_DIM + SSD_INNER,
                              Q_LORA + KV_LORA + ROPE_DIM + SSD_INNER + CONV_DIM,
                              Q_LORA + KV_LORA + ROPE_DIM + SSD_INNER + CONV_DIM + 2 * SSD_HEADS)
    w_kr = w_in[:, s1:s2]
    zcol = lambda n: jnp.zeros((D_MODEL, n), F32)
    wcat = jnp.concatenate(
        [w_in[:, :s0], w_in[:, s0:s1], w_in[:, s2:s3], w_in[:, s3:s4], w_in[:, s5:],
         w_kr, w_in[:, s4:s5], zcol(128 - ROPE_DIM - 2 * SSD_HEADS),
         _swap_halves(w_kr), zcol(128 - ROPE_DIM)], axis=1).astype(BF16)
    wqb = p["w_q_b"][i].reshape(Q_LORA, ATTN_HEADS, NOPE_DIM + ROPE_DIM)
    zq = jnp.zeros((Q_LORA, ATTN_HEADS, 128 - ROPE_DIM), F32)
    wq = jnp.concatenate([wqb, zq], axis=2).reshape(Q_LORA, ATTN_HEADS * QK_PAD).astype(BF16)
    wqs = jnp.concatenate([_swap_halves(wqb[:, :, NOPE_DIM:]), zq], axis=2).reshape(
        Q_LORA, ATTN_HEADS * 128).astype(BF16)
    wkv = p["w_kv_b"][i].reshape(KV_LORA, ATTN_HEADS, NOPE_DIM + V_DIM)
    wk = wkv[:, :, :NOPE_DIM].reshape(KV_LORA, ATTN_HEADS * NOPE_DIM).astype(BF16)
    wv = wkv[:, :, NOPE_DIM:].reshape(KV_LORA, ATTN_HEADS * V_DIM).T.astype(BF16)
    lane_pad = lambda f, b: jnp.concatenate(
        [jnp.zeros((DT_LANE,), F32), f, b, jnp.zeros((128 - DT_LANE - 2 * SSD_HEADS,), F32)])[None, :]
    w_out = p["w_out"][i].astype(BF16)
    return dict(
        n1g=p["norm1_g"][i][None, :], n2g=p["norm2_g"][i][None, :],
        wcat=wcat, qg=p["q_a_norm_g"][i][None, :], wq=wq, wqs=wqs,
        kvg=p["kv_a_norm_g"][i][None, :], wk=wk, wv=wv,
        conv_w=p["conv_w"][i], conv_b=p["conv_b"][i][None, :],
        dt_bias=lane_pad(p["dt_bias_fwd"][i], p["dt_bias_bwd"][i]),
        a_neg=lane_pad(-jnp.exp(p["a_log_fwd"][i]), -jnp.exp(p["a_log_bwd"][i])),
        d_skip=jnp.repeat(p["d_skip"][i], SSD_HEAD_DIM)[None, :], ssd_g=p["ssd_norm_g"][i][None, :],
        pool_w=p["pool_w"][i].astype(BF16), pool_scale=p["pool_scale"][i][None, :],
        wo_a=w_out[:ATTN_OUT], wo_s=w_out[ATTN_OUT:ATTN_OUT + SSD_INNER], wo_p=w_out[ATTN_OUT + SSD_INNER:],
        wpq=p["peer_wq"][i].astype(BF16),
        keys=p["peer_keys"][i].reshape(PEER_HEADS * 2, N_KEYS, PEER_HALF).astype(BF16),
        u=p["peer_u"][i].astype(BF16), v=p["peer_v"][i].astype(BF16),
    )


def _block(n, pref):
    for c in pref:
        if n % c == 0:
            return c
    raise ValueError(f"no block size in {pref} divides {n}")


def _encoder(x, mods, weights, fg):
    B, L, _ = x.shape
    cos_t, sin_t = _rope_tables(L)
    tm = _block(L, (256, 128))
    tq = _block(L, (256, 128))
    tk = _block(L // 2, (1024, 512, 256, 128))
    nchunk = _block(L // SSD_CHUNK, (4, 2, 1))
    lp = _block(L, (512, 256, 128))
    tb = _block(L, (512, 256, 128))
    tt = _block(B * L, (512, 256, 128))
    tg = _block(B * L, (64, 32, 16))
    for i in range(DEPTH):
        w, mod = weights[i], mods[i]
        q, k, v, z, xbc, pool_in, kd = _pre_mixer(x, mod, w, cos_t, sin_t, tm)
        attn = _attention(q, k, v, tq, tk)
        yf = _ssd_pass(xbc, kd, w, nchunk, reverse=False)
        ssd = _ssd_pass(xbc, kd, w, nchunk, reverse=True, yf=yf, z=z)
        pool = _pool(pool_in, w, lp)
        x1, h2, qp = _post_mixer(x, mod, attn, ssd, pool, w, tm)
        e, g = _peer_topk(qp.reshape(B * L, D_MODEL), w["keys"], tt)
        gates = _peer_gates(e, g, tg).reshape(B, L, N_EXPERTS)
        x = _peer_dense(h2, w["u"], w["v"], gates, x1, mod, fg, tb, 1024, last=(i == DEPTH - 1))
    return x


def kernel(x_prompt, x_sample, c_prompt, c_sample, mod_w, mod_b, norm1_g, norm2_g, w_in, q_a_norm_g, w_q_b, kv_a_norm_g, w_kv_b, conv_w, conv_b, a_log_fwd, a_log_bwd, dt_bias_fwd, dt_bias_bwd, d_skip, ssd_norm_g, pool_w, pool_scale, w_out, peer_wq, peer_keys, peer_u, peer_v, final_norm_g):
    p = dict(mod_w=mod_w, mod_b=mod_b, norm1_g=norm1_g, norm2_g=norm2_g, w_in=w_in,
             q_a_norm_g=q_a_norm_g, w_q_b=w_q_b, kv_a_norm_g=kv_a_norm_g, w_kv_b=w_kv_b,
             conv_w=conv_w, conv_b=conv_b, a_log_fwd=a_log_fwd, a_log_bwd=a_log_bwd,
             dt_bias_fwd=dt_bias_fwd, dt_bias_bwd=dt_bias_bwd, d_skip=d_skip, ssd_norm_g=ssd_norm_g,
             pool_w=pool_w, pool_scale=pool_scale, w_out=w_out, peer_wq=peer_wq, peer_keys=peer_keys,
             peer_u=peer_u, peer_v=peer_v)
    weights = [_layer_weights(p, i) for i in range(DEPTH)]
    bp, bs = c_prompt.shape[0], c_sample.shape[0]
    c_pad = jnp.concatenate([c_prompt, c_sample, jnp.zeros((8 - bp - bs, D_MODEL), F32)], axis=0)
    mods = [_modulation(c_pad, mod_w[i], mod_b[i][None, :]) for i in range(DEPTH)]
    fg = final_norm_g[None, :]
    mods_p = [m[:bp].reshape(bp, 6, D_MODEL) for m in mods]
    mods_s = [m[bp:bp + bs].reshape(bs, 6, D_MODEL) for m in mods]
    return (_encoder(x_prompt, mods_p, weights, fg), _encoder(x_sample, mods_s, weights, fg))
```

```python
import functools
import math

import jax
import jax.numpy as jnp
from jax import lax
from jax.experimental import pallas as pl
from jax.experimental.pallas import tpu as pltpu

F32 = jnp.float32
BF16 = jnp.bfloat16
I32 = jnp.int32

D_MODEL = 2048
DEPTH = 2
EPS = 1e-6
ATTN_HEADS = 8
Q_LORA = 512
KV_LORA = 256
NOPE_DIM = 128
ROPE_DIM = 64
V_DIM = 128
ROPE_BASE = 10000.0
QK_PAD = 256
SSD_HEADS = 8
SSD_HEAD_DIM = 64
SSD_INNER = SSD_HEADS * SSD_HEAD_DIM
SSD_GROUPS = 2
SSD_STATE = 128
SSD_CHUNK = 128
D_CONV = 4
CONV_DIM = SSD_INNER + 2 * SSD_GROUPS * SSD_STATE
POOL_WINDOWS = (2, 4, 8, 16)
POOL_GROUP = 128
POOL_DIM = len(POOL_WINDOWS) * POOL_GROUP
ATTN_OUT = ATTN_HEADS * V_DIM
PEER_HEADS = 8
N_KEYS = 128
N_EXPERTS = N_KEYS * N_KEYS
PEER_HALF = 128
PEER_TOPK = 16
HK = PEER_HEADS * PEER_TOPK
HALO = 8
DT_LANE = 64

C_CQ = 0
C_CKV = C_CQ + Q_LORA
C_Z = C_CKV + KV_LORA
C_XBC = C_Z + SSD_INNER
C_POOL = C_XBC + CONV_DIM
C_KD = C_POOL + POOL_DIM
C_KDS = C_KD + 128
C_END = C_KDS + 128

VMEM_LIMIT = 56 * 1024 * 1024

_STAIR = [(a, PEER_TOPK // (a + 1)) for a in range(PEER_TOPK)]
_STAIR_ROWS = sum(n for _, n in _STAIR)
_STAIR_PAD = ((_STAIR_ROWS + 7) // 8) * 8


def _cparams(*sem):
    return pltpu.CompilerParams(dimension_semantics=sem, vmem_limit_bytes=VMEM_LIMIT)


def _const_spec(shape):
    nd = len(shape)
    return pl.BlockSpec(shape, lambda *_: (0,) * nd, pipeline_mode=pl.Buffered(1))


def _rms(x, g):
    return x * lax.rsqrt(jnp.mean(x * x, axis=-1, keepdims=True) + EPS) * g


def _silu(x):
    return x * jax.nn.sigmoid(x)


def _gelu(x):
    return 0.5 * x * (1.0 + lax.erf(x * (1.0 / math.sqrt(2.0))))


def _dot(a, b):
    return jnp.dot(a, b, preferred_element_type=F32)


def _dot_nt(a, b):
    return lax.dot_general(a, b, (((1,), (1,)), ((), ())), preferred_element_type=F32)


def _mod_kernel(c_ref, w_ref, b_ref, o_ref):
    cs = _silu(c_ref[...])
    o_ref[...] = _dot(cs.astype(BF16), w_ref[...].astype(BF16)) + b_ref[...]


def _modulation(c_pad, mod_w, mod_b):
    bn = 1024
    n = mod_w.shape[1]
    return pl.pallas_call(
        _mod_kernel,
        grid=(n // bn,),
        in_specs=[pl.BlockSpec((8, D_MODEL), lambda j: (0, 0)),
                  pl.BlockSpec((D_MODEL, bn), lambda j: (0, j)),
                  pl.BlockSpec((1, bn), lambda j: (0, j))],
        out_specs=pl.BlockSpec((8, bn), lambda j: (0, j)),
        out_shape=jax.ShapeDtypeStruct((8, n), F32),
        compiler_params=_cparams("parallel"),
        name="modulation",
    )(c_pad, mod_w, mod_b)


def _pre_kernel(*refs, residual):
    if residual:
        pe_ref, pmod_ref, x_ref, *refs = refs
    else:
        x_ref, *refs = refs
    (mod_ref, n1g_ref, wcat_ref, qg_ref, wq_ref, wqs_ref, kvg_ref, wk_ref, wv_ref, cos_ref, sin_ref,
     q_ref, k_ref, v_ref, z_ref, xbc_ref, pool_ref, kd_ref, *xo_ref) = refs
    x = x_ref[0]
    if residual:
        x = x + pmod_ref[0, 5:6, :] * pe_ref[0]
        xo_ref[0][0] = x
    sh1 = mod_ref[0, 0:1, :]
    sc1 = mod_ref[0, 1:2, :]
    h = _rms(x, n1g_ref[...]) * (1.0 + sc1) + sh1
    proj = _dot(h.astype(BF16), wcat_ref[...])
    z_ref[0] = proj[:, C_Z:C_XBC]
    xbc_ref[0] = proj[:, C_XBC:C_POOL]
    pool_ref[0] = proj[:, C_POOL:C_KD]
    kd = proj[:, C_KD:C_KDS]
    kd_ref[0] = kd
    cos = cos_ref[...]
    sin = sin_ref[...]
    scale = math.log2(math.e) / math.sqrt(NOPE_DIM + ROPE_DIM)

    cqn = _rms(proj[:, C_CQ:C_CKV], qg_ref[...]).astype(BF16)
    qm = _dot(cqn, wq_ref[...])
    qs = _dot(cqn, wqs_ref[...])
    for hd in range(ATTN_HEADS):
        o = hd * QK_PAD
        q_ref[0, hd, :, 0:NOPE_DIM] = (qm[:, o:o + NOPE_DIM] * scale).astype(BF16)
        rope = qm[:, o + NOPE_DIM:o + QK_PAD] * cos + qs[:, hd * 128:(hd + 1) * 128] * sin
        q_ref[0, hd, :, NOPE_DIM:QK_PAD] = (rope * scale).astype(BF16)

    ckvn = _rms(proj[:, C_CKV:C_Z], kvg_ref[...]).astype(BF16)
    kn = _dot(ckvn, wk_ref[...])
    vvt = _dot_nt(wv_ref[...], ckvn)
    krope = (kd * cos + proj[:, C_KDS:C_END] * sin).astype(BF16)
    for hd in range(ATTN_HEADS):
        k_ref[0, hd, :, 0:NOPE_DIM] = kn[:, hd * NOPE_DIM:(hd + 1) * NOPE_DIM].astype(BF16)
        k_ref[0, hd, :, NOPE_DIM:QK_PAD] = krope
        v_ref[0, hd] = vvt[hd * V_DIM:(hd + 1) * V_DIM, :].astype(BF16)


def _pre_mixer(x, mod, w, cos_t, sin_t, tm, peer=None, prev_mod=None):
    B, L, _ = x.shape
    grid = (B, L // tm)
    tok = lambda b, i: (b, i, 0)
    head = lambda b, i: (b, 0, i, 0)
    modspec = pl.BlockSpec((1, 6, D_MODEL), lambda b, i: (b, 0, 0))
    residual = peer is not None
    res_specs = [pl.BlockSpec((1, tm, D_MODEL), tok), modspec] if residual else []
    res_args = [peer, prev_mod] if residual else []
    res_out_specs = [pl.BlockSpec((1, tm, D_MODEL), tok)] if residual else []
    res_out_shape = [jax.ShapeDtypeStruct((B, L, D_MODEL), F32)] if residual else []
    return pl.pallas_call(
        functools.partial(_pre_kernel, residual=residual),
        grid=grid,
        in_specs=res_specs + [
                  pl.BlockSpec((1, tm, D_MODEL), tok),
                  modspec,
                  _const_spec((1, D_MODEL)),
                  _const_spec((D_MODEL, C_END)),
                  _const_spec((1, Q_LORA)),
                  _const_spec((Q_LORA, ATTN_HEADS * QK_PAD)),
                  _const_spec((Q_LORA, ATTN_HEADS * 128)),
                  _const_spec((1, KV_LORA)),
                  _const_spec((KV_LORA, ATTN_HEADS * NOPE_DIM)),
                  _const_spec((ATTN_HEADS * V_DIM, KV_LORA)),
                  pl.BlockSpec((tm, 128), lambda b, i: (i, 0)),
                  pl.BlockSpec((tm, 128), lambda b, i: (i, 0))],
        out_specs=[pl.BlockSpec((1, ATTN_HEADS, tm, QK_PAD), head),
                   pl.BlockSpec((1, ATTN_HEADS, tm, QK_PAD), head),
                   pl.BlockSpec((1, ATTN_HEADS, V_DIM, tm), lambda b, i: (b, 0, 0, i)),
                   pl.BlockSpec((1, tm, SSD_INNER), tok),
                   pl.BlockSpec((1, tm, CONV_DIM), tok),
                   pl.BlockSpec((1, tm, POOL_DIM), tok),
                   pl.BlockSpec((1, tm, 128), tok)] + res_out_specs,
        out_shape=[jax.ShapeDtypeStruct((B, ATTN_HEADS, L, QK_PAD), BF16),
                   jax.ShapeDtypeStruct((B, ATTN_HEADS, L, QK_PAD), BF16),
                   jax.ShapeDtypeStruct((B, ATTN_HEADS, V_DIM, L), BF16),
                   jax.ShapeDtypeStruct((B, L, SSD_INNER), F32),
                   jax.ShapeDtypeStruct((B, L, CONV_DIM), F32),
                   jax.ShapeDtypeStruct((B, L, POOL_DIM), F32),
                   jax.ShapeDtypeStruct((B, L, 128), F32)] + res_out_shape,
        compiler_params=_cparams("parallel", "parallel"),
        name="pre_mixer",
    )(*res_args, x, mod, w["n1g"], w["wcat"], w["qg"], w["wq"], w["wqs"], w["kvg"], w["wk"], w["wv"],
      cos_t, sin_t)


def _attn_kernel(q_ref, k_ref, vt_ref, o_ref, st_sc, p_sc, *, tk, nk):
    q = q_ref[0, 0]
    tq = q.shape[0]

    def scores(j, slot):
        off = pl.multiple_of(j * tk, tk)
        st = _dot_nt(k_ref[0, 0, pl.ds(off, tk), :], q)
        st_sc[slot] = st
        return jnp.max(st, axis=0, keepdims=True)

    def weighted_values(j, slot):
        off = pl.multiple_of(j * tk, tk)
        return _dot(vt_ref[0, 0, :, pl.ds(off, tk)], p_sc[slot])

    def step(j, cur, carry):
        m_prev, l_prev, acc, a_prev, cmax = carry
        nxt = 1 - cur
        acc_new = a_prev * acc + weighted_values(jnp.maximum(j - 1, 0), nxt)
        cmax_next = scores(jnp.minimum(j + 1, nk - 1), nxt)
        m_new = jnp.maximum(m_prev, cmax)
        alpha = jnp.exp2(m_prev - m_new)
        p = jnp.exp2(st_sc[cur] - m_new)
        l_new = alpha * l_prev + jnp.sum(p, axis=0, keepdims=True)
        p_sc[cur] = p.astype(BF16)
        return m_new, l_new, acc_new, alpha, cmax_next

    def body(jj, carry):
        return step(2 * jj + 1, 1, step(2 * jj, 0, carry))

    cmax0 = scores(0, 0)
    p_sc[1] = jnp.zeros((tk, tq), BF16)
    init = (jnp.full((1, tq), -jnp.inf, F32), jnp.zeros((1, tq), F32), jnp.zeros((V_DIM, tq), F32),
            jnp.ones((1, tq), F32), cmax0)
    _, l, acc, a_last, _ = lax.fori_loop(0, nk // 2, body, init)
    acc = a_last * acc + weighted_values(nk - 1, 1)
    o_ref[0] = (acc / l).T.astype(o_ref.dtype)


def _attention(q, k, vt, tq, tk):
    B, H, L, _ = q.shape
    nk = L // tk
    assert nk % 2 == 0, (L, tk)
    kern = functools.partial(_attn_kernel, tk=tk, nk=nk)
    return pl.pallas_call(
        kern,
        grid=(B, H, L // tq),
        in_specs=[pl.BlockSpec((1, 1, tq, QK_PAD), lambda b, h, i: (b, h, i, 0)),
                  pl.BlockSpec((1, 1, L, QK_PAD), lambda b, h, i: (b, h, 0, 0)),
                  pl.BlockSpec((1, 1, V_DIM, L), lambda b, h, i: (b, h, 0, 0))],
        out_specs=pl.BlockSpec((1, tq, V_DIM), lambda b, h, i: (b, i, h)),
        out_shape=jax.ShapeDtypeStruct((B, L, H * V_DIM), BF16),
        scratch_shapes=[pltpu.VMEM((2, tk, tq), F32), pltpu.VMEM((2, tk, tq), BF16)],
        compiler_params=_cparams("parallel", "parallel", "arbitrary"),
        name="attention",
    )(q, k, vt)


def _halo_ext(prev_ref, cur_ref, next_ref, i, nblk):
    prev = jnp.where(i > 0, prev_ref[0], 0.0)
    nxt = jnp.where(i < nblk - 1, next_ref[0], 0.0)
    return jnp.concatenate([prev, cur_ref[0], nxt], axis=0)


def _shift_rows(ext, d, lb):
    n = ext.shape[0]
    r = ext if d == 0 else pltpu.roll(ext, (-d) % n, 0)
    return r[HALO:HALO + lb]


def _ssd_kernel(*refs, reverse, final, nblk, nchunk):
    if final:
        (xp_ref, xc_ref, xn_ref, kd_ref, cw_ref, cb_ref, dtb_ref, a_ref, yf_ref, z_ref, dsk_ref, ng_ref,
         o_ref, st_sc) = refs
    else:
        xp_ref, xc_ref, xn_ref, kd_ref, cw_ref, cb_ref, dtb_ref, a_ref, o_ref, st_sc = refs
    step = pl.program_id(1)
    blk = (nblk - 1 - step) if reverse else step
    lb = nchunk * SSD_CHUNK

    @pl.when(step == 0)
    def _():
        st_sc[...] = jnp.zeros(st_sc.shape, F32)

    ext = _halo_ext(xp_ref, xc_ref, xn_ref, blk, nblk)
    conv = cb_ref[...] + sum(cw_ref[kk:kk + 1, :] * _shift_rows(ext, kk - D_CONV // 2, lb) for kk in range(D_CONV))
    act = _silu(conv)
    dt_all = jax.nn.softplus(kd_ref[0] + dtb_ref[...])
    da_all = dt_all * a_ref[...]

    row = lax.broadcasted_iota(I32, (SSD_CHUNK, SSD_CHUNK), 0)
    col = lax.broadcasted_iota(I32, (SSD_CHUNK, SSD_CHUNK), 1)
    keep = (row <= col) if reverse else (row >= col)
    tri = jnp.where(keep, 1.0, 0.0).astype(F32)
    lane = lax.broadcasted_iota(I32, (SSD_CHUNK, 128), 1)
    low = lane < SSD_HEAD_DIM
    dlane = DT_LANE + (SSD_HEADS if reverse else 0)
    edge = 0 if reverse else SSD_CHUNK - 1

    def pair(arr, h0):
        return jnp.where(low, arr[:, dlane + h0:dlane + h0 + 1], arr[:, dlane + h0 + 1:dlane + h0 + 2])

    ys = []
    for cc in (range(nchunk - 1, -1, -1) if reverse else range(nchunk)):
        r0 = cc * SSD_CHUNK
        da = da_all[r0:r0 + SSD_CHUNK]
        dt = dt_all[r0:r0 + SSD_CHUNK]
        acum = jnp.dot(tri, da, preferred_element_type=F32, precision=lax.Precision.HIGHEST)
        acum_t = acum.T
        a_edge = acum[edge:edge + 1, :]
        ycols = [None] * (SSD_HEADS // 2)
        for g in range(SSD_GROUPS):
            bg = act[r0:r0 + SSD_CHUNK, SSD_INNER + g * SSD_STATE:SSD_INNER + (g + 1) * SSD_STATE]
            cg = act[r0:r0 + SSD_CHUNK, SSD_INNER + (SSD_GROUPS + g) * SSD_STATE:
                     SSD_INNER + (SSD_GROUPS + g + 1) * SSD_STATE]
            cgb = cg.astype(BF16)
            cb = _dot_nt(cgb, bg.astype(BF16))
            bgt = bg.T.astype(BF16)
            for pp in range(SSD_HEADS // SSD_GROUPS // 2):
                pi = g * 2 + pp
                h0 = 2 * pi
                xdt = act[r0:r0 + SSD_CHUNK, pi * 128:(pi + 1) * 128] * pair(dt, h0)
                xdtb = xdt.astype(BF16)
                yd = []
                for hh in (h0, h0 + 1):
                    seg = acum[:, dlane + hh:dlane + hh + 1] - acum_t[dlane + hh:dlane + hh + 1, :]
                    decay = jnp.exp(jnp.where(keep, seg, -jnp.inf))
                    yd.append(_dot((cb * decay).astype(BF16), xdtb))
                y_diag = jnp.where(low, yd[0], yd[1])
                st = st_sc[pi]
                y_off = _dot(cgb, st.astype(BF16)) * jnp.exp(pair(acum, h0))
                ycols[pi] = y_diag + y_off
                to_edge = jnp.exp(pair(a_edge - acum, h0))
                st_sc[pi] = st * jnp.exp(pair(a_edge, h0)) + _dot(bgt, (xdt * to_edge).astype(BF16))
        ys.append((cc, jnp.concatenate(ycols, axis=1)))
    y = jnp.concatenate([v for _, v in sorted(ys, key=lambda t: t[0])], axis=0)

    if final:
        tot = yf_ref[0] + y + act[:, 0:SSD_INNER] * dsk_ref[...]
        o_ref[0] = _rms(tot * _silu(z_ref[0]), ng_ref[...]).astype(o_ref.dtype)
    else:
        o_ref[0] = y


def _ssd_pass(xbc, kd, w, nchunk, reverse, yf=None, z=None):
    B, L, _ = xbc.shape
    lb = nchunk * SSD_CHUNK
    nblk = L // lb
    hb = lb // HALO
    final = yf is not None
    bi = (lambda i: nblk - 1 - i) if reverse else (lambda i: i)
    cur = lambda b, i: (b, bi(i), 0)
    prv = lambda b, i: (b, jnp.maximum(bi(i) * hb - 1, 0), 0)
    nxt = lambda b, i: (b, jnp.minimum((bi(i) + 1) * hb, L // HALO - 1), 0)
    in_specs = [pl.BlockSpec((1, HALO, CONV_DIM), prv),
                pl.BlockSpec((1, lb, CONV_DIM), cur),
                pl.BlockSpec((1, HALO, CONV_DIM), nxt),
                pl.BlockSpec((1, lb, 128), cur),
                _const_spec((D_CONV, CONV_DIM)), _const_spec((1, CONV_DIM)),
                _const_spec((1, 128)), _const_spec((1, 128))]
    args = [xbc, xbc, xbc, kd, w["conv_w"], w["conv_b"], w["dt_bias"], w["a_neg"]]
    if final:
        in_specs += [pl.BlockSpec((1, lb, SSD_INNER), cur), pl.BlockSpec((1, lb, SSD_INNER), cur),
                     _const_spec((1, SSD_INNER)), _const_spec((1, SSD_INNER))]
        args += [yf, z, w["d_skip"], w["ssd_g"]]
    kern = functools.partial(_ssd_kernel, reverse=reverse, final=final, nblk=nblk, nchunk=nchunk)
    return pl.pallas_call(
        kern,
        grid=(B, nblk),
        in_specs=in_specs,
        out_specs=pl.BlockSpec((1, lb, SSD_INNER), cur),
        out_shape=jax.ShapeDtypeStruct((B, L, SSD_INNER), BF16 if final else F32),
        scratch_shapes=[pltpu.VMEM((SSD_HEADS // 2, SSD_STATE, 128), F32)],
        compiler_params=_cparams("parallel", "arbitrary"),
        name="ssd_bwd" if reverse else "ssd_fwd",
    )(*args)


def _pool_kernel(up_ref, uc_ref, un_ref, pw_ref, ps_ref, o_ref, *, nblk, lb, seq):
    blk = pl.program_id(1)
    ext = _halo_ext(up_ref, uc_ref, un_ref, blk, nblk)
    n = ext.shape[0]
    t = blk * lb + lax.broadcasted_iota(I32, (lb, 1), 0)
    for gi, wdw in enumerate(POOL_WINDOWS):
        e = ext[:, gi * POOL_GROUP:(gi + 1) * POOL_GROUP]
        run, width = e, 1
        while width < wdw:
            run = run + pltpu.roll(run, width, 0)
            width *= 2
        tot = _shift_rows(run, wdw // 2 - 1, lb)
        cnt = jnp.minimum(t + wdw // 2, seq) - jnp.maximum(t - wdw // 2, 0)
        d = tot / cnt.astype(F32) - e[HALO:HALO + lb]
        y = _dot(d.astype(BF16), pw_ref[gi])
        o_ref[0, :, gi * POOL_GROUP:(gi + 1) * POOL_GROUP] = (
            y * ps_ref[:, gi * POOL_GROUP:(gi + 1) * POOL_GROUP]).astype(o_ref.dtype)


def _pool(u, w, lb):
    B, L, _ = u.shape
    nblk = L // lb
    hb = lb // HALO
    cur = lambda b, i: (b, i, 0)
    prv = lambda b, i: (b, jnp.maximum(i * hb - 1, 0), 0)
    nxt = lambda b, i: (b, jnp.minimum((i + 1) * hb, L // HALO - 1), 0)
    kern = functools.partial(_pool_kernel, nblk=nblk, lb=lb, seq=L)
    return pl.pallas_call(
        kern,
        grid=(B, nblk),
        in_specs=[pl.BlockSpec((1, HALO, POOL_DIM), prv),
                  pl.BlockSpec((1, lb, POOL_DIM), cur),
                  pl.BlockSpec((1, HALO, POOL_DIM), nxt),
                  _const_spec((len(POOL_WINDOWS), POOL_GROUP, POOL_GROUP)),
                  _const_spec((1, POOL_DIM))],
        out_specs=pl.BlockSpec((1, lb, POOL_DIM), cur),
        out_shape=jax.ShapeDtypeStruct((B, L, POOL_DIM), BF16),
        compiler_params=_cparams("parallel", "parallel"),
        name="pool",
    )(u, u, u, w["pool_w"], w["pool_scale"])


def _post_kernel(x_ref, mod_ref, at_ref, sd_ref, po_ref, woa_ref, wos_ref, wop_ref, n2g_ref, wpq_ref,
                 x1_ref, h2_ref, qp_ref):
    g1 = mod_ref[0, 2:3, :]
    sh2 = mod_ref[0, 3:4, :]
    sc2 = mod_ref[0, 4:5, :]
    mix = _dot(at_ref[0], woa_ref[...]) + _dot(sd_ref[0], wos_ref[...]) + _dot(po_ref[0], wop_ref[...])
    x1 = x_ref[0] + g1 * mix
    x1_ref[0] = x1
    h2 = (_rms(x1, n2g_ref[...]) * (1.0 + sc2) + sh2).astype(BF16)
    h2_ref[0] = h2
    qp_ref[0] = _dot(h2, wpq_ref[...]).astype(BF16)


def _post_mixer(x, mod, attn, ssd, pool, w, tm):
    B, L, _ = x.shape
    tok = lambda b, i: (b, i, 0)
    return pl.pallas_call(
        _post_kernel,
        grid=(B, L // tm),
        in_specs=[pl.BlockSpec((1, tm, D_MODEL), tok),
                  pl.BlockSpec((1, 6, D_MODEL), lambda b, i: (b, 0, 0)),
                  pl.BlockSpec((1, tm, ATTN_OUT), tok),
                  pl.BlockSpec((1, tm, SSD_INNER), tok),
                  pl.BlockSpec((1, tm, POOL_DIM), tok),
                  _const_spec((ATTN_OUT, D_MODEL)), _const_spec((SSD_INNER, D_MODEL)),
                  _const_spec((POOL_DIM, D_MODEL)), _const_spec((1, D_MODEL)),
                  _const_spec((D_MODEL, D_MODEL))],
        out_specs=[pl.BlockSpec((1, tm, D_MODEL), tok)] * 3,
        out_shape=[jax.ShapeDtypeStruct((B, L, D_MODEL), F32),
                   jax.ShapeDtypeStruct((B, L, D_MODEL), BF16),
                   jax.ShapeDtypeStruct((B, L, D_MODEL), BF16)],
        compiler_params=_cparams("parallel", "parallel"),
        name="post_mixer",
    )(x, mod, attn, ssd, pool, w["wo_a"], w["wo_s"], w["wo_p"], w["n2g"], w["wpq"])


def _top16(s, out_v, out_i):
    n = s.shape[0]
    rows = lax.broadcasted_iota(I32, s.shape, 0).astype(F32)
    for r in range(PEER_TOPK):
        m = jnp.max(s, axis=0, keepdims=True)
        idx = jnp.min(jnp.where(s == m, rows, float(n)), axis=0, keepdims=True)
        out_v[r:r + 1, :] = m
        out_i[r:r + 1, :] = idx
        s = jnp.where(rows == idx, -jnp.inf, s)


def _topk_kernel(qp_ref, keys_ref, e_ref, g_ref, v1_sc, i1_sc, v2_sc, i2_sc, cv_sc, ci_sc, tv_sc, ti_sc,
                 eo_sc, go_sc):
    cv_sc[...] = jnp.full(cv_sc.shape, -jnp.inf, F32)
    ci_sc[...] = jnp.zeros(ci_sc.shape, F32)
    for hd in range(PEER_HEADS):
        for half, (vs, is_) in enumerate(((v1_sc, i1_sc), (v2_sc, i2_sc))):
            c0 = (hd * 2 + half) * PEER_HALF
            s = _dot_nt(keys_ref[hd * 2 + half], qp_ref[:, c0:c0 + PEER_HALF])
            _top16(s, vs, is_)
        off = 0
        for a, nb in _STAIR:
            cv_sc[off:off + nb, :] = v1_sc[a:a + 1, :] + v2_sc[0:nb, :]
            ci_sc[off:off + nb, :] = i1_sc[a:a + 1, :] * float(N_KEYS) + i2_sc[0:nb, :]
            off += nb
        cand = cv_sc[...]
        cidx = ci_sc[...]
        rows = lax.broadcasted_iota(I32, cand.shape, 0).astype(F32)
        for r in range(PEER_TOPK):
            m = jnp.max(cand, axis=0, keepdims=True)
            pos = jnp.min(jnp.where(cand == m, rows, float(_STAIR_PAD)), axis=0, keepdims=True)
            hit = rows == pos
            tv_sc[r:r + 1, :] = m
            ti_sc[r:r + 1, :] = jnp.sum(jnp.where(hit, cidx, 0.0), axis=0, keepdims=True)
            cand = jnp.where(hit, -jnp.inf, cand)
        tv = tv_sc[...]
        p = jnp.exp(tv - tv[0:1, :])
        go_sc[hd * PEER_TOPK:(hd + 1) * PEER_TOPK, :] = p / jnp.sum(p, axis=0, keepdims=True)
        eo_sc[hd * PEER_TOPK:(hd + 1) * PEER_TOPK, :] = ti_sc[...]
    e_ref[...] = eo_sc[...].T.astype(I32)
    g_ref[...] = go_sc[...].T


def _peer_topk(qp, keys, tm):
    T = qp.shape[0]
    return pl.pallas_call(
        _topk_kernel,
        grid=(T // tm,),
        in_specs=[pl.BlockSpec((tm, D_MODEL), lambda i: (i, 0)),
                  _const_spec((PEER_HEADS * 2, N_KEYS, PEER_HALF))],
        out_specs=[pl.BlockSpec((tm, HK), lambda i: (i, 0))] * 2,
        out_shape=[jax.ShapeDtypeStruct((T, HK), I32), jax.ShapeDtypeStruct((T, HK), F32)],
        scratch_shapes=[pltpu.VMEM((PEER_TOPK, tm), F32)] * 4
                       + [pltpu.VMEM((_STAIR_PAD, tm), F32)] * 2
                       + [pltpu.VMEM((PEER_TOPK, tm), F32)] * 2
                       + [pltpu.VMEM((HK, tm), F32)] * 2,
        compiler_params=_cparams("parallel"),
        name="peer_topk",
    )(qp, keys)


G_PITCH = N_KEYS + 8


def _gate_kernel(e_ref, g_ref, o_ref, gs_sc, *, tg):
    sub = lax.broadcasted_iota(I32, (N_KEYS, HK), 0)

    def per_token(t, carry):
        er = e_ref[pl.ds(t, 1), :]
        gr = g_ref[pl.ds(t, 1), :]
        i1 = lax.shift_right_logical(er, 7)
        i2 = lax.bitwise_and(er, N_KEYS - 1)
        a = jnp.where(sub == i1, 1.0, 0.0).astype(BF16)
        b = jnp.where(sub == i2, gr, 0.0).astype(BF16)
        gs_sc[pl.ds(pl.multiple_of(t * G_PITCH, 8), N_KEYS), :] = _dot_nt(a, b)
        return carry

    lax.fori_loop(0, tg, per_token, 0, unroll=16)

    def per_key(c, carry):
        col = pl.multiple_of(c * N_KEYS, N_KEYS)
        for t0 in range(0, tg, 16):
            lo = gs_sc[pl.ds(t0 * G_PITCH + c, 8, stride=G_PITCH), :]
            hi = gs_sc[pl.ds((t0 + 8) * G_PITCH + c, 8, stride=G_PITCH), :]
            o_ref[t0:t0 + 16, pl.ds(col, N_KEYS)] = jnp.concatenate([lo, hi], axis=0).astype(o_ref.dtype)
        return carry

    lax.fori_loop(0, N_KEYS, per_key, 0, unroll=2)


def _peer_gates(e, g, tg):
    T = e.shape[0]
    kern = functools.partial(_gate_kernel, tg=tg)
    return pl.pallas_call(
        kern,
        grid=(T // tg,),
        in_specs=[pl.BlockSpec((tg, HK), lambda i: (i, 0))] * 2,
        out_specs=pl.BlockSpec((tg, N_EXPERTS), lambda i: (i, 0)),
        out_shape=jax.ShapeDtypeStruct((T, N_EXPERTS), BF16),
        scratch_shapes=[pltpu.VMEM((tg * G_PITCH, N_KEYS), F32)],
        compiler_params=_cparams("parallel"),
        name="peer_gates",
    )(e, g)


def _dense_kernel(h_ref, u_ref, v_ref, gt_ref, o_ref):
    @pl.when(pl.program_id(2) == 0)
    def _():
        o_ref[...] = jnp.zeros(o_ref.shape, F32)

    a = _gelu(_dot_nt(h_ref[0], u_ref[...]))
    wgt = (a * gt_ref[0].astype(F32)).astype(BF16)
    o_ref[0] += _dot(wgt, v_ref[...])


def _peer_dense(h2, u, v, gates, tb, eb):
    B, L, _ = h2.shape
    tok = lambda b, i, j: (b, i, 0)
    return pl.pallas_call(
        _dense_kernel,
        grid=(B, L // tb, N_EXPERTS // eb),
        in_specs=[pl.BlockSpec((1, tb, D_MODEL), tok),
                  pl.BlockSpec((eb, D_MODEL), lambda b, i, j: (j, 0)),
                  pl.BlockSpec((eb, D_MODEL), lambda b, i, j: (j, 0)),
                  pl.BlockSpec((1, tb, eb), lambda b, i, j: (b, i, j))],
        out_specs=pl.BlockSpec((1, tb, D_MODEL), tok),
        out_shape=jax.ShapeDtypeStruct((B, L, D_MODEL), F32),
        compiler_params=_cparams("parallel", "parallel", "arbitrary"),
        name="peer_dense",
    )(h2, u, v, gates)


def _final_kernel(x1_ref, pe_ref, mod_ref, fg_ref, o_ref):
    o_ref[0] = _rms(x1_ref[0] + mod_ref[0, 5:6, :] * pe_ref[0], fg_ref[...])


def _final_norm(x1, peer, mod, fg, tm):
    B, L, _ = x1.shape
    tok = lambda b, i: (b, i, 0)
    return pl.pallas_call(
        _final_kernel,
        grid=(B, L // tm),
        in_specs=[pl.BlockSpec((1, tm, D_MODEL), tok), pl.BlockSpec((1, tm, D_MODEL), tok),
                  pl.BlockSpec((1, 6, D_MODEL), lambda b, i: (b, 0, 0)), _const_spec((1, D_MODEL))],
        out_specs=pl.BlockSpec((1, tm, D_MODEL), tok),
        out_shape=jax.ShapeDtypeStruct((B, L, D_MODEL), F32),
        compiler_params=_cparams("parallel", "parallel"),
        name="final_norm",
    )(x1, peer, mod, fg)


def _rope_tables(L):
    inv = 1.0 / (ROPE_BASE ** (jnp.arange(0, ROPE_DIM, 2, dtype=F32) / ROPE_DIM))
    ang = jnp.arange(L, dtype=F32)[:, None] * inv[None, :]
    cos, sin = jnp.cos(ang), jnp.sin(ang)
    zero = jnp.zeros((L, 128 - ROPE_DIM), F32)
    return jnp.concatenate([cos, cos, zero], axis=1), jnp.concatenate([-sin, sin, zero], axis=1)


def _swap_halves(w):
    half = w.shape[-1] // 2
    return jnp.concatenate([w[..., half:], w[..., :half]], axis=-1)


def _layer_weights(p, i):
    w_in = p["w_in"][i]
    s0, s1, s2, s3, s4, s5 = (Q_LORA, Q_LORA + KV_LORA, Q_LORA + KV_LORA + ROPE_DIM,
                              Q_LORA + KV_LORA + ROPE_DIM + SSD_INNER,
                              Q_LORA + KV_LORA + ROPE_DIM + SSD_INNER + CONV_DIM,
                              Q_LORA + KV_LORA + ROPE_DIM + SSD_INNER + CONV_DIM + 2 * SSD_HEADS)
    w_kr = w_in[:, s1:s2]
    zcol = lambda n: jnp.zeros((D_MODEL, n), F32)
    wcat = jnp.concatenate(
        [w_in[:, :s0], w_in[:, s0:s1], w_in[:, s2:s3], w_in[:, s3:s4], w_in[:, s5:],
         w_kr, w_in[:, s4:s5], zcol(128 - ROPE_DIM - 2 * SSD_HEADS),
         _swap_halves(w_kr), zcol(128 - ROPE_DIM)], axis=1).astype(BF16)
    wqb = p["w_q_b"][i].reshape(Q_LORA, ATTN_HEADS, NOPE_DIM + ROPE_DIM)
    zq = jnp.zeros((Q_LORA, ATTN_HEADS, 128 - ROPE_DIM), F32)
    wq = jnp.concatenate([wqb, zq], axis=2).reshape(Q_LORA, ATTN_HEADS * QK_PAD).astype(BF16)
    wqs = jnp.concatenate([_swap_halves(wqb[:, :, NOPE_DIM:]), zq], axis=2).reshape(
        Q_LORA, ATTN_HEADS * 128).astype(BF16)
    wkv = p["w_kv_b"][i].reshape(KV_LORA, ATTN_HEADS, NOPE_DIM + V_DIM)
    wk = wkv[:, :, :NOPE_DIM].reshape(KV_LORA, ATTN_HEADS * NOPE_DIM).astype(BF16)
    wv = wkv[:, :, NOPE_DIM:].reshape(KV_LORA, ATTN_HEADS * V_DIM).T.astype(BF16)
    lane_pad = lambda f, b: jnp.concatenate(
        [jnp.zeros((DT_LANE,), F32), f, b, jnp.zeros((128 - DT_LANE - 2 * SSD_HEADS,), F32)])[None, :]
    w_out = p["w_out"][i].astype(BF16)
    return dict(
        n1g=p["norm1_g"][i][None, :], n2g=p["norm2_g"][i][None, :],
        wcat=wcat, qg=p["q_a_norm_g"][i][None, :], wq=wq, wqs=wqs,
        kvg=p["kv_a_norm_g"][i][None, :], wk=wk, wv=wv,
        conv_w=p["conv_w"][i], conv_b=p["conv_b"][i][None, :],
        dt_bias=lane_pad(p["dt_bias_fwd"][i], p["dt_bias_bwd"][i]),
        a_neg=lane_pad(-jnp.exp(p["a_log_fwd"][i]), -jnp.exp(p["a_log_bwd"][i])),
        d_skip=jnp.repeat(p["d_skip"][i], SSD_HEAD_DIM)[None, :], ssd_g=p["ssd_norm_g"][i][None, :],
        pool_w=p["pool_w"][i].astype(BF16), pool_scale=p["pool_scale"][i][None, :],
        wo_a=w_out[:ATTN_OUT], wo_s=w_out[ATTN_OUT:ATTN_OUT + SSD_INNER], wo_p=w_out[ATTN_OUT + SSD_INNER:],
        wpq=p["peer_wq"][i].astype(BF16),
        keys=p["peer_keys"][i].reshape(PEER_HEADS * 2, N_KEYS, PEER_HALF).astype(BF16),
        u=p["peer_u"][i].astype(BF16), v=p["peer_v"][i].astype(BF16),
    )


def _block(n, pref):
    for c in pref:
        if n % c == 0:
            return c
    raise ValueError(f"no block size in {pref} divides {n}")


def _encoder(x, mods, weights, fg):
    B, L, _ = x.shape
    cos_t, sin_t = _rope_tables(L)
    tm = _block(L, (256, 128))
    tq = _block(L, (256, 128))
    tk = _block(L // 2, (1024, 512, 256, 128))
    nchunk = _block(L // SSD_CHUNK, (4, 2, 1))
    lp = _block(L, (512, 256, 128))
    tb = _block(L, (1024, 512, 256, 128))
    tt = _block(B * L, (512, 256, 128))
    tg = _block(B * L, (64, 32, 16))
    peer = None
    for i in range(DEPTH):
        w, mod = weights[i], mods[i]
        if peer is None:
            q, k, v, z, xbc, pool_in, kd = _pre_mixer(x, mod, w, cos_t, sin_t, tm)
        else:
            q, k, v, z, xbc, pool_in, kd, x = _pre_mixer(x1, mod, w, cos_t, sin_t, tm, peer, mods[i - 1])
        attn = _attention(q, k, v, tq, tk)
        yf = _ssd_pass(xbc, kd, w, nchunk, reverse=False)
        ssd = _ssd_pass(xbc, kd, w, nchunk, reverse=True, yf=yf, z=z)
        pool = _pool(pool_in, w, lp)
        x1, h2, qp = _post_mixer(x, mod, attn, ssd, pool, w, tm)
        e, g = _peer_topk(qp.reshape(B * L, D_MODEL), w["keys"], tt)
        gates = _peer_gates(e, g, tg).reshape(B, L, N_EXPERTS)
        peer = _peer_dense(h2, w["u"], w["v"], gates, tb, 1024)
    return _final_norm(x1, peer, mods[DEPTH - 1], fg, tm)


def kernel(x_prompt, x_sample, c_prompt, c_sample, mod_w, mod_b, norm1_g, norm2_g, w_in, q_a_norm_g, w_q_b, kv_a_norm_g, w_kv_b, conv_w, conv_b, a_log_fwd, a_log_bwd, dt_bias_fwd, dt_bias_bwd, d_skip, ssd_norm_g, pool_w, pool_scale, w_out, peer_wq, peer_keys, peer_u, peer_v, final_norm_g):
    p = dict(mod_w=mod_w, mod_b=mod_b, norm1_g=norm1_g, norm2_g=norm2_g, w_in=w_in,
             q_a_norm_g=q_a_norm_g, w_q_b=w_q_b, kv_a_norm_g=kv_a_norm_g, w_kv_b=w_kv_b,
             conv_w=conv_w, conv_b=conv_b, a_log_fwd=a_log_fwd, a_log_bwd=a_log_bwd,
             dt_bias_fwd=dt_bias_fwd, dt_bias_bwd=dt_bias_bwd, d_skip=d_skip, ssd_norm_g=ssd_norm_g,
             pool_w=pool_w, pool_scale=pool_scale, w_out=w_out, peer_wq=peer_wq, peer_keys=peer_keys,
             peer_u=peer_u, peer_v=peer_v)
    weights = [_layer_weights(p, i) for i in range(DEPTH)]
    bp, bs = c_prompt.shape[0], c_sample.shape[0]
    c_pad = jnp.concatenate([c_prompt, c_sample, jnp.zeros((8 - bp - bs, D_MODEL), F32)], axis=0)
    mods = [_modulation(c_pad, mod_w[i], mod_b[i][None, :]) for i in range(DEPTH)]
    fg = final_norm_g[None, :]
    mods_p = [m[:bp].reshape(bp, 6, D_MODEL) for m in mods]
    mods_s = [m[bp:bp + bs].reshape(bs, 6, D_MODEL) for m in mods]
    return (_encoder(x_prompt, mods_p, weights, fg), _encoder(x_sample, mods_s, weights, fg))
```

```python
import functools
import math

import jax
import jax.numpy as jnp
from jax import lax
from jax.experimental import pallas as pl
from jax.experimental.pallas import tpu as pltpu

F32 = jnp.float32
BF16 = jnp.bfloat16
I32 = jnp.int32

D_MODEL = 2048
DEPTH = 2
EPS = 1e-6
ATTN_HEADS = 8
Q_LORA = 512
KV_LORA = 256
NOPE_DIM = 128
ROPE_DIM = 64
V_DIM = 128
ROPE_BASE = 10000.0
QK_PAD = 256
SSD_HEADS = 8
SSD_HEAD_DIM = 64
SSD_INNER = SSD_HEADS * SSD_HEAD_DIM
SSD_GROUPS = 2
SSD_STATE = 128
SSD_CHUNK = 128
D_CONV = 4
CONV_DIM = SSD_INNER + 2 * SSD_GROUPS * SSD_STATE
POOL_WINDOWS = (2, 4, 8, 16)
POOL_GROUP = 128
POOL_DIM = len(POOL_WINDOWS) * POOL_GROUP
ATTN_OUT = ATTN_HEADS * V_DIM
PEER_HEADS = 8
N_KEYS = 128
N_EXPERTS = N_KEYS * N_KEYS
PEER_HALF = 128
PEER_TOPK = 16
HK = PEER_HEADS * PEER_TOPK
HALO = 8
DT_LANE = 64

C_CQ = 0
C_CKV = C_CQ + Q_LORA
C_Z = C_CKV + KV_LORA
C_XBC = C_Z + SSD_INNER
C_POOL = C_XBC + CONV_DIM
C_KD = C_POOL + POOL_DIM
C_KDS = C_KD + 128
C_END = C_KDS + 128

VMEM_LIMIT = 56 * 1024 * 1024

_STAIR = [(a, PEER_TOPK // (a + 1)) for a in range(PEER_TOPK)]
_STAIR_ROWS = sum(n for _, n in _STAIR)
_STAIR_PAD = ((_STAIR_ROWS + 7) // 8) * 8


def _cparams(*sem):
    return pltpu.CompilerParams(dimension_semantics=sem, vmem_limit_bytes=VMEM_LIMIT)


def _const_spec(shape):
    nd = len(shape)
    return pl.BlockSpec(shape, lambda *_: (0,) * nd, pipeline_mode=pl.Buffered(1))


def _rms(x, g):
    return x * lax.rsqrt(jnp.mean(x * x, axis=-1, keepdims=True) + EPS) * g


def _silu(x):
    return x * jax.nn.sigmoid(x)


def _gelu(x):
    return 0.5 * x * (1.0 + lax.erf(x * (1.0 / math.sqrt(2.0))))


def _dot(a, b):
    return jnp.dot(a, b, preferred_element_type=F32)


def _dot_nt(a, b):
    return lax.dot_general(a, b, (((1,), (1,)), ((), ())), preferred_element_type=F32)


def _mod_kernel(c_ref, w_ref, b_ref, o_ref):
    cs = _silu(c_ref[...])
    o_ref[...] = _dot(cs.astype(BF16), w_ref[...].astype(BF16)) + b_ref[...]


def _modulation(c_pad, mod_w, mod_b):
    bn = 1024
    n = mod_w.shape[1]
    return pl.pallas_call(
        _mod_kernel,
        grid=(n // bn,),
        in_specs=[pl.BlockSpec((8, D_MODEL), lambda j: (0, 0)),
                  pl.BlockSpec((D_MODEL, bn), lambda j: (0, j)),
                  pl.BlockSpec((1, bn), lambda j: (0, j))],
        out_specs=pl.BlockSpec((8, bn), lambda j: (0, j)),
        out_shape=jax.ShapeDtypeStruct((8, n), F32),
        compiler_params=_cparams("parallel"),
        name="modulation",
    )(c_pad, mod_w, mod_b)


def _pre_kernel(*refs, residual):
    if residual:
        pe_ref, pmod_ref, x_ref, *refs = refs
    else:
        x_ref, *refs = refs
    (mod_ref, n1g_ref, wcat_ref, qg_ref, wq_ref, wqs_ref, kvg_ref, wk_ref, wv_ref, cos_ref, sin_ref,
     q_ref, k_ref, v_ref, z_ref, xbc_ref, pool_ref, kd_ref, *xo_ref) = refs
    x = x_ref[0]
    if residual:
        x = x + pmod_ref[0, 5:6, :] * pe_ref[0]
        xo_ref[0][0] = x
    sh1 = mod_ref[0, 0:1, :]
    sc1 = mod_ref[0, 1:2, :]
    h = _rms(x, n1g_ref[...]) * (1.0 + sc1) + sh1
    proj = _dot(h.astype(BF16), wcat_ref[...])
    z_ref[0] = proj[:, C_Z:C_XBC]
    xbc_ref[0] = proj[:, C_XBC:C_POOL]
    pool_ref[0] = proj[:, C_POOL:C_KD]
    kd = proj[:, C_KD:C_KDS]
    kd_ref[0] = kd
    cos = cos_ref[...]
    sin = sin_ref[...]
    scale = math.log2(math.e) / math.sqrt(NOPE_DIM + ROPE_DIM)

    cqn = _rms(proj[:, C_CQ:C_CKV], qg_ref[...]).astype(BF16)
    qm = _dot(cqn, wq_ref[...])
    qs = _dot(cqn, wqs_ref[...])
    for hd in range(ATTN_HEADS):
        o = hd * QK_PAD
        q_ref[0, hd, :, 0:NOPE_DIM] = (qm[:, o:o + NOPE_DIM] * scale).astype(BF16)
        rope = qm[:, o + NOPE_DIM:o + QK_PAD] * cos + qs[:, hd * 128:(hd + 1) * 128] * sin
        q_ref[0, hd, :, NOPE_DIM:QK_PAD] = (rope * scale).astype(BF16)

    ckvn = _rms(proj[:, C_CKV:C_Z], kvg_ref[...]).astype(BF16)
    kn = _dot(ckvn, wk_ref[...])
    vvt = _dot_nt(wv_ref[...], ckvn)
    krope = (kd * cos + proj[:, C_KDS:C_END] * sin).astype(BF16)
    for hd in range(ATTN_HEADS):
        k_ref[0, hd, :, 0:NOPE_DIM] = kn[:, hd * NOPE_DIM:(hd + 1) * NOPE_DIM].astype(BF16)
        k_ref[0, hd, :, NOPE_DIM:QK_PAD] = krope
        v_ref[0, hd] = vvt[hd * V_DIM:(hd + 1) * V_DIM, :].astype(BF16)


def _pre_mixer(x, mod, w, cos_t, sin_t, tm, peer=None, prev_mod=None):
    B, L, _ = x.shape
    grid = (B, L // tm)
    tok = lambda b, i: (b, i, 0)
    head = lambda b, i: (b, 0, i, 0)
    modspec = pl.BlockSpec((1, 6, D_MODEL), lambda b, i: (b, 0, 0))
    residual = peer is not None
    res_specs = [pl.BlockSpec((1, tm, D_MODEL), tok), modspec] if residual else []
    res_args = [peer, prev_mod] if residual else []
    res_out_specs = [pl.BlockSpec((1, tm, D_MODEL), tok)] if residual else []
    res_out_shape = [jax.ShapeDtypeStruct((B, L, D_MODEL), F32)] if residual else []
    return pl.pallas_call(
        functools.partial(_pre_kernel, residual=residual),
        grid=grid,
        in_specs=res_specs + [
                  pl.BlockSpec((1, tm, D_MODEL), tok),
                  modspec,
                  _const_spec((1, D_MODEL)),
                  _const_spec((D_MODEL, C_END)),
                  _const_spec((1, Q_LORA)),
                  _const_spec((Q_LORA, ATTN_HEADS * QK_PAD)),
                  _const_spec((Q_LORA, ATTN_HEADS * 128)),
                  _const_spec((1, KV_LORA)),
                  _const_spec((KV_LORA, ATTN_HEADS * NOPE_DIM)),
                  _const_spec((ATTN_HEADS * V_DIM, KV_LORA)),
                  pl.BlockSpec((tm, 128), lambda b, i: (i, 0)),
                  pl.BlockSpec((tm, 128), lambda b, i: (i, 0))],
        out_specs=[pl.BlockSpec((1, ATTN_HEADS, tm, QK_PAD), head),
                   pl.BlockSpec((1, ATTN_HEADS, tm, QK_PAD), head),
                   pl.BlockSpec((1, ATTN_HEADS, V_DIM, tm), lambda b, i: (b, 0, 0, i)),
                   pl.BlockSpec((1, tm, SSD_INNER), tok),
                   pl.BlockSpec((1, tm, CONV_DIM), tok),
                   pl.BlockSpec((1, tm, POOL_DIM), tok),
                   pl.BlockSpec((1, tm, 128), tok)] + res_out_specs,
        out_shape=[jax.ShapeDtypeStruct((B, ATTN_HEADS, L, QK_PAD), BF16),
                   jax.ShapeDtypeStruct((B, ATTN_HEADS, L, QK_PAD), BF16),
                   jax.ShapeDtypeStruct((B, ATTN_HEADS, V_DIM, L), BF16),
                   jax.ShapeDtypeStruct((B, L, SSD_INNER), F32),
                   jax.ShapeDtypeStruct((B, L, CONV_DIM), F32),
                   jax.ShapeDtypeStruct((B, L, POOL_DIM), F32),
                   jax.ShapeDtypeStruct((B, L, 128), F32)] + res_out_shape,
        compiler_params=_cparams("parallel", "parallel"),
        name="pre_mixer",
    )(*res_args, x, mod, w["n1g"], w["wcat"], w["qg"], w["wq"], w["wqs"], w["kvg"], w["wk"], w["wv"],
      cos_t, sin_t)


ATTN_SLOTS = 4


def _attn_kernel(q_ref, k_ref, vt_ref, o_ref, st_sc, p_sc, *, tk, nk):
    q = q_ref[0, 0]
    tq = q.shape[0]
    ntrip = nk // ATTN_SLOTS

    def chunk_start(j):
        return j * tk if isinstance(j, int) else pl.multiple_of(j * tk, tk)

    def scores(j):
        st = _dot_nt(k_ref[0, 0, pl.ds(chunk_start(j), tk), :], q)
        return st, jnp.max(st, axis=0, keepdims=True)

    def weighted_values(j, slot):
        return _dot(vt_ref[0, 0, :, pl.ds(chunk_start(j), tk)], p_sc[slot])

    def trip(jj, carry, first, last):
        for u in range(ATTN_SLOTS):
            j = ATTN_SLOTS * jj + u
            m_prev, l_prev, acc, a_prev, cmax, cmax_1 = carry
            if not (first and u == 0):
                acc = a_prev * acc + weighted_values(j - 1, (u - 1) % 2)
            if last and u >= ATTN_SLOTS - 2:
                cmax_2 = cmax_1
            else:
                st, cmax_2 = scores(j + 2)
                st_sc[(u + 2) % ATTN_SLOTS] = st
            m_new = jnp.maximum(m_prev, cmax)
            alpha = jnp.exp2(m_prev - m_new)
            p = jnp.exp2(st_sc[u] - m_new)
            l_new = alpha * l_prev + jnp.sum(p, axis=0, keepdims=True)
            p_sc[u % 2] = p.astype(BF16)
            carry = (m_new, l_new, acc, alpha, cmax_1, cmax_2)
        return carry

    st_sc[0], cmax0 = scores(0)
    st_sc[1], cmax1 = scores(1)
    carry = (jnp.full((1, tq), -jnp.inf, F32), jnp.zeros((1, tq), F32), jnp.zeros((V_DIM, tq), F32),
             jnp.ones((1, tq), F32), cmax0, cmax1)
    carry = trip(0, carry, True, ntrip == 1)
    if ntrip > 2:
        carry = lax.fori_loop(1, ntrip - 1, lambda jj, c: trip(jj, c, False, False), carry)
    if ntrip > 1:
        carry = trip(ntrip - 1, carry, False, True)
    _, l, acc, a_last, _, _ = carry
    acc = a_last * acc + weighted_values(nk - 1, (ATTN_SLOTS - 1) % 2)
    o_ref[0] = (acc / l).T.astype(o_ref.dtype)


def _attention(q, k, vt, tq, tk):
    B, H, L, _ = q.shape
    nk = L // tk
    assert nk % ATTN_SLOTS == 0, (L, tk)
    kern = functools.partial(_attn_kernel, tk=tk, nk=nk)
    return pl.pallas_call(
        kern,
        grid=(B, H, L // tq),
        in_specs=[pl.BlockSpec((1, 1, tq, QK_PAD), lambda b, h, i: (b, h, i, 0)),
                  pl.BlockSpec((1, 1, L, QK_PAD), lambda b, h, i: (b, h, 0, 0)),
                  pl.BlockSpec((1, 1, V_DIM, L), lambda b, h, i: (b, h, 0, 0))],
        out_specs=pl.BlockSpec((1, tq, V_DIM), lambda b, h, i: (b, i, h)),
        out_shape=jax.ShapeDtypeStruct((B, L, H * V_DIM), BF16),
        scratch_shapes=[pltpu.VMEM((ATTN_SLOTS, tk, tq), F32), pltpu.VMEM((2, tk, tq), BF16)],
        compiler_params=_cparams("parallel", "parallel", "arbitrary"),
        name="attention",
    )(q, k, vt)


def _halo_ext(prev_ref, cur_ref, next_ref, i, nblk):
    prev = jnp.where(i > 0, prev_ref[0], 0.0)
    nxt = jnp.where(i < nblk - 1, next_ref[0], 0.0)
    return jnp.concatenate([prev, cur_ref[0], nxt], axis=0)


def _shift_rows(ext, d, lb):
    n = ext.shape[0]
    r = ext if d == 0 else pltpu.roll(ext, (-d) % n, 0)
    return r[HALO:HALO + lb]


def _ssd_kernel(*refs, reverse, final, nblk, nchunk):
    if final:
        (xp_ref, xc_ref, xn_ref, kd_ref, cw_ref, cb_ref, dtb_ref, a_ref, yf_ref, z_ref, dsk_ref, ng_ref,
         o_ref, st_sc) = refs
    else:
        xp_ref, xc_ref, xn_ref, kd_ref, cw_ref, cb_ref, dtb_ref, a_ref, o_ref, st_sc = refs
    step = pl.program_id(1)
    blk = (nblk - 1 - step) if reverse else step
    lb = nchunk * SSD_CHUNK

    @pl.when(step == 0)
    def _():
        st_sc[...] = jnp.zeros(st_sc.shape, F32)

    ext = _halo_ext(xp_ref, xc_ref, xn_ref, blk, nblk)
    conv = cb_ref[...] + sum(cw_ref[kk:kk + 1, :] * _shift_rows(ext, kk - D_CONV // 2, lb) for kk in range(D_CONV))
    act = _silu(conv)
    dt_all = jax.nn.softplus(kd_ref[0] + dtb_ref[...])
    da_all = dt_all * a_ref[...]

    row = lax.broadcasted_iota(I32, (SSD_CHUNK, SSD_CHUNK), 0)
    col = lax.broadcasted_iota(I32, (SSD_CHUNK, SSD_CHUNK), 1)
    keep = (row <= col) if reverse else (row >= col)
    tri = jnp.where(keep, 1.0, 0.0).astype(F32)
    lane = lax.broadcasted_iota(I32, (SSD_CHUNK, 128), 1)
    low = lane < SSD_HEAD_DIM
    dlane = DT_LANE + (SSD_HEADS if reverse else 0)
    edge = 0 if reverse else SSD_CHUNK - 1

    def pair(arr, h0):
        return jnp.where(low, arr[:, dlane + h0:dlane + h0 + 1], arr[:, dlane + h0 + 1:dlane + h0 + 2])

    ys = []
    for cc in (range(nchunk - 1, -1, -1) if reverse else range(nchunk)):
        r0 = cc * SSD_CHUNK
        da = da_all[r0:r0 + SSD_CHUNK]
        dt = dt_all[r0:r0 + SSD_CHUNK]
        acum = jnp.dot(tri, da, preferred_element_type=F32, precision=lax.Precision.HIGHEST)
        acum_t = acum.T
        a_edge = acum[edge:edge + 1, :]
        ycols = [None] * (SSD_HEADS // 2)
        for g in range(SSD_GROUPS):
            bg = act[r0:r0 + SSD_CHUNK, SSD_INNER + g * SSD_STATE:SSD_INNER + (g + 1) * SSD_STATE]
            cg = act[r0:r0 + SSD_CHUNK, SSD_INNER + (SSD_GROUPS + g) * SSD_STATE:
                     SSD_INNER + (SSD_GROUPS + g + 1) * SSD_STATE]
            cgb = cg.astype(BF16)
            cb = _dot_nt(cgb, bg.astype(BF16))
            bgt = bg.T.astype(BF16)
            for pp in range(SSD_HEADS // SSD_GROUPS // 2):
                pi = g * 2 + pp
                h0 = 2 * pi
                xdt = act[r0:r0 + SSD_CHUNK, pi * 128:(pi + 1) * 128] * pair(dt, h0)
                xdtb = xdt.astype(BF16)
                yd = []
                for hh in (h0, h0 + 1):
                    seg = acum[:, dlane + hh:dlane + hh + 1] - acum_t[dlane + hh:dlane + hh + 1, :]
                    decay = jnp.exp(jnp.where(keep, seg, -jnp.inf))
                    yd.append(_dot((cb * decay).astype(BF16), xdtb))
                y_diag = jnp.where(low, yd[0], yd[1])
                st = st_sc[pi]
                y_off = _dot(cgb, st.astype(BF16)) * jnp.exp(pair(acum, h0))
                ycols[pi] = y_diag + y_off
                to_edge = jnp.exp(pair(a_edge - acum, h0))
                st_sc[pi] = st * jnp.exp(pair(a_edge, h0)) + _dot(bgt, (xdt * to_edge).astype(BF16))
        ys.append((cc, jnp.concatenate(ycols, axis=1)))
    y = jnp.concatenate([v for _, v in sorted(ys, key=lambda t: t[0])], axis=0)

    if final:
        tot = yf_ref[0] + y + act[:, 0:SSD_INNER] * dsk_ref[...]
        o_ref[0] = _rms(tot * _silu(z_ref[0]), ng_ref[...]).astype(o_ref.dtype)
    else:
        o_ref[0] = y


def _ssd_pass(xbc, kd, w, nchunk, reverse, yf=None, z=None):
    B, L, _ = xbc.shape
    lb = nchunk * SSD_CHUNK
    nblk = L // lb
    hb = lb // HALO
    final = yf is not None
    bi = (lambda i: nblk - 1 - i) if reverse else (lambda i: i)
    cur = lambda b, i: (b, bi(i), 0)
    prv = lambda b, i: (b, jnp.maximum(bi(i) * hb - 1, 0), 0)
    nxt = lambda b, i: (b, jnp.minimum((bi(i) + 1) * hb, L // HALO - 1), 0)
    in_specs = [pl.BlockSpec((1, HALO, CONV_DIM), prv),
                pl.BlockSpec((1, lb, CONV_DIM), cur),
                pl.BlockSpec((1, HALO, CONV_DIM), nxt),
                pl.BlockSpec((1, lb, 128), cur),
                _const_spec((D_CONV, CONV_DIM)), _const_spec((1, CONV_DIM)),
                _const_spec((1, 128)), _const_spec((1, 128))]
    args = [xbc, xbc, xbc, kd, w["conv_w"], w["conv_b"], w["dt_bias"], w["a_neg"]]
    if final:
        in_specs += [pl.BlockSpec((1, lb, SSD_INNER), cur), pl.BlockSpec((1, lb, SSD_INNER), cur),
                     _const_spec((1, SSD_INNER)), _const_spec((1, SSD_INNER))]
        args += [yf, z, w["d_skip"], w["ssd_g"]]
    kern = functools.partial(_ssd_kernel, reverse=reverse, final=final, nblk=nblk, nchunk=nchunk)
    return pl.pallas_call(
        kern,
        grid=(B, nblk),
        in_specs=in_specs,
        out_specs=pl.BlockSpec((1, lb, SSD_INNER), cur),
        out_shape=jax.ShapeDtypeStruct((B, L, SSD_INNER), BF16 if final else F32),
        scratch_shapes=[pltpu.VMEM((SSD_HEADS // 2, SSD_STATE, 128), F32)],
        compiler_params=_cparams("parallel", "arbitrary"),
        name="ssd_bwd" if reverse else "ssd_fwd",
    )(*args)


def _pool_kernel(up_ref, uc_ref, un_ref, pw_ref, ps_ref, o_ref, *, nblk, lb, seq):
    blk = pl.program_id(1)
    ext = _halo_ext(up_ref, uc_ref, un_ref, blk, nblk)
    n = ext.shape[0]
    t = blk * lb + lax.broadcasted_iota(I32, (lb, 1), 0)
    for gi, wdw in enumerate(POOL_WINDOWS):
        e = ext[:, gi * POOL_GROUP:(gi + 1) * POOL_GROUP]
        run, width = e, 1
        while width < wdw:
            run = run + pltpu.roll(run, width, 0)
            width *= 2
        tot = _shift_rows(run, wdw // 2 - 1, lb)
        cnt = jnp.minimum(t + wdw // 2, seq) - jnp.maximum(t - wdw // 2, 0)
        d = tot / cnt.astype(F32) - e[HALO:HALO + lb]
        y = _dot(d.astype(BF16), pw_ref[gi])
        o_ref[0, :, gi * POOL_GROUP:(gi + 1) * POOL_GROUP] = (
            y * ps_ref[:, gi * POOL_GROUP:(gi + 1) * POOL_GROUP]).astype(o_ref.dtype)


def _pool(u, w, lb):
    B, L, _ = u.shape
    nblk = L // lb
    hb = lb // HALO
    cur = lambda b, i: (b, i, 0)
    prv = lambda b, i: (b, jnp.maximum(i * hb - 1, 0), 0)
    nxt = lambda b, i: (b, jnp.minimum((i + 1) * hb, L // HALO - 1), 0)
    kern = functools.partial(_pool_kernel, nblk=nblk, lb=lb, seq=L)
    return pl.pallas_call(
        kern,
        grid=(B, nblk),
        in_specs=[pl.BlockSpec((1, HALO, POOL_DIM), prv),
                  pl.BlockSpec((1, lb, POOL_DIM), cur),
                  pl.BlockSpec((1, HALO, POOL_DIM), nxt),
                  _const_spec((len(POOL_WINDOWS), POOL_GROUP, POOL_GROUP)),
                  _const_spec((1, POOL_DIM))],
        out_specs=pl.BlockSpec((1, lb, POOL_DIM), cur),
        out_shape=jax.ShapeDtypeStruct((B, L, POOL_DIM), BF16),
        compiler_params=_cparams("parallel", "parallel"),
        name="pool",
    )(u, u, u, w["pool_w"], w["pool_scale"])


def _post_kernel(x_ref, mod_ref, at_ref, sd_ref, po_ref, woa_ref, wos_ref, wop_ref, n2g_ref, wpq_ref,
                 x1_ref, h2_ref, qp_ref):
    g1 = mod_ref[0, 2:3, :]
    sh2 = mod_ref[0, 3:4, :]
    sc2 = mod_ref[0, 4:5, :]
    mix = _dot(at_ref[0], woa_ref[...]) + _dot(sd_ref[0], wos_ref[...]) + _dot(po_ref[0], wop_ref[...])
    x1 = x_ref[0] + g1 * mix
    x1_ref[0] = x1
    h2 = (_rms(x1, n2g_ref[...]) * (1.0 + sc2) + sh2).astype(BF16)
    h2_ref[0] = h2
    qp_ref[0] = _dot(h2, wpq_ref[...]).astype(BF16)


def _post_mixer(x, mod, attn, ssd, pool, w, tm):
    B, L, _ = x.shape
    tok = lambda b, i: (b, i, 0)
    return pl.pallas_call(
        _post_kernel,
        grid=(B, L // tm),
        in_specs=[pl.BlockSpec((1, tm, D_MODEL), tok),
                  pl.BlockSpec((1, 6, D_MODEL), lambda b, i: (b, 0, 0)),
                  pl.BlockSpec((1, tm, ATTN_OUT), tok),
                  pl.BlockSpec((1, tm, SSD_INNER), tok),
                  pl.BlockSpec((1, tm, POOL_DIM), tok),
                  _const_spec((ATTN_OUT, D_MODEL)), _const_spec((SSD_INNER, D_MODEL)),
                  _const_spec((POOL_DIM, D_MODEL)), _const_spec((1, D_MODEL)),
                  _const_spec((D_MODEL, D_MODEL))],
        out_specs=[pl.BlockSpec((1, tm, D_MODEL), tok)] * 3,
        out_shape=[jax.ShapeDtypeStruct((B, L, D_MODEL), F32),
                   jax.ShapeDtypeStruct((B, L, D_MODEL), BF16),
                   jax.ShapeDtypeStruct((B, L, D_MODEL), BF16)],
        compiler_params=_cparams("parallel", "parallel"),
        name="post_mixer",
    )(x, mod, attn, ssd, pool, w["wo_a"], w["wo_s"], w["wo_p"], w["n2g"], w["wpq"])


def _top16(s, out_v, out_i):
    n = s.shape[0]
    rows = lax.broadcasted_iota(I32, s.shape, 0).astype(F32)
    for r in range(PEER_TOPK):
        m = jnp.max(s, axis=0, keepdims=True)
        idx = jnp.min(jnp.where(s == m, rows, float(n)), axis=0, keepdims=True)
        out_v[r:r + 1, :] = m
        out_i[r:r + 1, :] = idx
        s = jnp.where(rows == idx, -jnp.inf, s)


def _topk_kernel(qp_ref, keys_ref, e_ref, g_ref, v1_sc, i1_sc, v2_sc, i2_sc, cv_sc, ci_sc, tv_sc, ti_sc,
                 eo_sc, go_sc):
    cv_sc[...] = jnp.full(cv_sc.shape, -jnp.inf, F32)
    ci_sc[...] = jnp.zeros(ci_sc.shape, F32)
    for hd in range(PEER_HEADS):
        for half, (vs, is_) in enumerate(((v1_sc, i1_sc), (v2_sc, i2_sc))):
            c0 = (hd * 2 + half) * PEER_HALF
            s = _dot_nt(keys_ref[hd * 2 + half], qp_ref[:, c0:c0 + PEER_HALF])
            _top16(s, vs, is_)
        off = 0
        for a, nb in _STAIR:
            cv_sc[off:off + nb, :] = v1_sc[a:a + 1, :] + v2_sc[0:nb, :]
            ci_sc[off:off + nb, :] = i1_sc[a:a + 1, :] * float(N_KEYS) + i2_sc[0:nb, :]
            off += nb
        cand = cv_sc[...]
        cidx = ci_sc[...]
        rows = lax.broadcasted_iota(I32, cand.shape, 0).astype(F32)
        for r in range(PEER_TOPK):
            m = jnp.max(cand, axis=0, keepdims=True)
            pos = jnp.min(jnp.where(cand == m, rows, float(_STAIR_PAD)), axis=0, keepdims=True)
            hit = rows == pos
            tv_sc[r:r + 1, :] = m
            ti_sc[r:r + 1, :] = jnp.sum(jnp.where(hit, cidx, 0.0), axis=0, keepdims=True)
            cand = jnp.where(hit, -jnp.inf, cand)
        tv = tv_sc[...]
        p = jnp.exp(tv - tv[0:1, :])
        go_sc[hd * PEER_TOPK:(hd + 1) * PEER_TOPK, :] = p / jnp.sum(p, axis=0, keepdims=True)
        eo_sc[hd * PEER_TOPK:(hd + 1) * PEER_TOPK, :] = ti_sc[...]
    e_ref[...] = eo_sc[...].T.astype(I32)
    g_ref[...] = go_sc[...].T


def _peer_topk(qp, keys, tm):
    T = qp.shape[0]
    return pl.pallas_call(
        _topk_kernel,
        grid=(T // tm,),
        in_specs=[pl.BlockSpec((tm, D_MODEL), lambda i: (i, 0)),
                  _const_spec((PEER_HEADS * 2, N_KEYS, PEER_HALF))],
        out_specs=[pl.BlockSpec((tm, HK), lambda i: (i, 0))] * 2,
        out_shape=[jax.ShapeDtypeStruct((T, HK), I32), jax.ShapeDtypeStruct((T, HK), F32)],
        scratch_shapes=[pltpu.VMEM((PEER_TOPK, tm), F32)] * 4
                       + [pltpu.VMEM((_STAIR_PAD, tm), F32)] * 2
                       + [pltpu.VMEM((PEER_TOPK, tm), F32)] * 2
                       + [pltpu.VMEM((HK, tm), F32)] * 2,
        compiler_params=_cparams("parallel"),
        name="peer_topk",
    )(qp, keys)


G_PITCH = N_KEYS + 8


def _gate_kernel(e_ref, g_ref, o_ref, gs_sc, *, tg):
    sub = lax.broadcasted_iota(I32, (N_KEYS, HK), 0)

    def per_token(t, carry):
        er = e_ref[pl.ds(t, 1), :]
        gr = g_ref[pl.ds(t, 1), :]
        i1 = lax.shift_right_logical(er, 7)
        i2 = lax.bitwise_and(er, N_KEYS - 1)
        a = jnp.where(sub == i1, 1.0, 0.0).astype(BF16)
        b = jnp.where(sub == i2, gr, 0.0).astype(BF16)
        gs_sc[pl.ds(pl.multiple_of(t * G_PITCH, 8), N_KEYS), :] = _dot_nt(a, b)
        return carry

    lax.fori_loop(0, tg, per_token, 0, unroll=16)

    def per_key(c, carry):
        col = pl.multiple_of(c * N_KEYS, N_KEYS)
        for t0 in range(0, tg, 16):
            lo = gs_sc[pl.ds(t0 * G_PITCH + c, 8, stride=G_PITCH), :]
            hi = gs_sc[pl.ds((t0 + 8) * G_PITCH + c, 8, stride=G_PITCH), :]
            o_ref[t0:t0 + 16, pl.ds(col, N_KEYS)] = jnp.concatenate([lo, hi], axis=0).astype(o_ref.dtype)
        return carry

    lax.fori_loop(0, N_KEYS, per_key, 0, unroll=2)


def _peer_gates(e, g, tg):
    T = e.shape[0]
    kern = functools.partial(_gate_kernel, tg=tg)
    return pl.pallas_call(
        kern,
        grid=(T // tg,),
        in_specs=[pl.BlockSpec((tg, HK), lambda i: (i, 0))] * 2,
        out_specs=pl.BlockSpec((tg, N_EXPERTS), lambda i: (i, 0)),
        out_shape=jax.ShapeDtypeStruct((T, N_EXPERTS), BF16),
        scratch_shapes=[pltpu.VMEM((tg * G_PITCH, N_KEYS), F32)],
        compiler_params=_cparams("parallel"),
        name="peer_gates",
    )(e, g)


def _dense_kernel(h_ref, u_ref, v_ref, gt_ref, o_ref):
    @pl.when(pl.program_id(2) == 0)
    def _():
        o_ref[...] = jnp.zeros(o_ref.shape, F32)

    a = _gelu(_dot_nt(h_ref[0], u_ref[...]))
    wgt = (a * gt_ref[0].astype(F32)).astype(BF16)
    o_ref[0] += _dot(wgt, v_ref[...])


def _peer_dense(h2, u, v, gates, tb, eb):
    B, L, _ = h2.shape
    tok = lambda b, i, j: (b, i, 0)
    return pl.pallas_call(
        _dense_kernel,
        grid=(B, L // tb, N_EXPERTS // eb),
        in_specs=[pl.BlockSpec((1, tb, D_MODEL), tok),
                  pl.BlockSpec((eb, D_MODEL), lambda b, i, j: (j, 0)),
                  pl.BlockSpec((eb, D_MODEL), lambda b, i, j: (j, 0)),
                  pl.BlockSpec((1, tb, eb), lambda b, i, j: (b, i, j))],
        out_specs=pl.BlockSpec((1, tb, D_MODEL), tok),
        out_shape=jax.ShapeDtypeStruct((B, L, D_MODEL), F32),
        compiler_params=_cparams("parallel", "parallel", "arbitrary"),
        name="peer_dense",
    )(h2, u, v, gates)


def _final_kernel(x1_ref, pe_ref, mod_ref, fg_ref, o_ref):
    o_ref[0] = _rms(x1_ref[0] + mod_ref[0, 5:6, :] * pe_ref[0], fg_ref[...])


def _final_norm(x1, peer, mod, fg, tm):
    B, L, _ = x1.shape
    tok = lambda b, i: (b, i, 0)
    return pl.pallas_call(
        _final_kernel,
        grid=(B, L // tm),
        in_specs=[pl.BlockSpec((1, tm, D_MODEL), tok), pl.BlockSpec((1, tm, D_MODEL), tok),
                  pl.BlockSpec((1, 6, D_MODEL), lambda b, i: (b, 0, 0)), _const_spec((1, D_MODEL))],
        out_specs=pl.BlockSpec((1, tm, D_MODEL), tok),
        out_shape=jax.ShapeDtypeStruct((B, L, D_MODEL), F32),
        compiler_params=_cparams("parallel", "parallel"),
        name="final_norm",
    )(x1, peer, mod, fg)


def _rope_tables(L):
    inv = 1.0 / (ROPE_BASE ** (jnp.arange(0, ROPE_DIM, 2, dtype=F32) / ROPE_DIM))
    ang = jnp.arange(L, dtype=F32)[:, None] * inv[None, :]
    cos, sin = jnp.cos(ang), jnp.sin(ang)
    zero = jnp.zeros((L, 128 - ROPE_DIM), F32)
    return jnp.concatenate([cos, cos, zero], axis=1), jnp.concatenate([-sin, sin, zero], axis=1)


def _swap_halves(w):
    half = w.shape[-1] // 2
    return jnp.concatenate([w[..., half:], w[..., :half]], axis=-1)


def _layer_weights(p, i):
    w_in = p["w_in"][i]
    s0, s1, s2, s3, s4, s5 = (Q_LORA, Q_LORA + KV_LORA, Q_LORA + KV_LORA + ROPE_DIM,
                              Q_LORA + KV_LORA + ROPE_DIM + SSD_INNER,
                              Q_LORA + KV_LORA + ROPE_DIM + SSD_INNER + CONV_DIM,
                              Q_LORA + KV_LORA + ROPE_DIM + SSD_INNER + CONV_DIM + 2 * SSD_HEADS)
    w_kr = w_in[:, s1:s2]
    zcol = lambda n: jnp.zeros((D_MODEL, n), F32)
    wcat = jnp.concatenate(
        [w_in[:, :s0], w_in[:, s0:s1], w_in[:, s2:s3], w_in[:, s3:s4], w_in[:, s5:],
         w_kr, w_in[:, s4:s5], zcol(128 - ROPE_DIM - 2 * SSD_HEADS),
         _swap_halves(w_kr), zcol(128 - ROPE_DIM)], axis=1).astype(BF16)
    wqb = p["w_q_b"][i].reshape(Q_LORA, ATTN_HEADS, NOPE_DIM + ROPE_DIM)
    zq = jnp.zeros((Q_LORA, ATTN_HEADS, 128 - ROPE_DIM), F32)
    wq = jnp.concatenate([wqb, zq], axis=2).reshape(Q_LORA, ATTN_HEADS * QK_PAD).astype(BF16)
    wqs = jnp.concatenate([_swap_halves(wqb[:, :, NOPE_DIM:]), zq], axis=2).reshape(
        Q_LORA, ATTN_HEADS * 128).astype(BF16)
    wkv = p["w_kv_b"][i].reshape(KV_LORA, ATTN_HEADS, NOPE_DIM + V_DIM)
    wk = wkv[:, :, :NOPE_DIM].reshape(KV_LORA, ATTN_HEADS * NOPE_DIM).astype(BF16)
    wv = wkv[:, :, NOPE_DIM:].reshape(KV_LORA, ATTN_HEADS * V_DIM).T.astype(BF16)
    lane_pad = lambda f, b: jnp.concatenate(
        [jnp.zeros((DT_LANE,), F32), f, b, jnp.zeros((128 - DT_LANE - 2 * SSD_HEADS,), F32)])[None, :]
    w_out = p["w_out"][i].astype(BF16)
    return dict(
        n1g=p["norm1_g"][i][None, :], n2g=p["norm2_g"][i][None, :],
        wcat=wcat, qg=p["q_a_norm_g"][i][None, :], wq=wq, wqs=wqs,
        kvg=p["kv_a_norm_g"][i][None, :], wk=wk, wv=wv,
        conv_w=p["conv_w"][i], conv_b=p["conv_b"][i][None, :],
        dt_bias=lane_pad(p["dt_bias_fwd"][i], p["dt_bias_bwd"][i]),
        a_neg=lane_pad(-jnp.exp(p["a_log_fwd"][i]), -jnp.exp(p["a_log_bwd"][i])),
        d_skip=jnp.repeat(p["d_skip"][i], SSD_HEAD_DIM)[None, :], ssd_g=p["ssd_norm_g"][i][None, :],
        pool_w=p["pool_w"][i].astype(BF16), pool_scale=p["pool_scale"][i][None, :],
        wo_a=w_out[:ATTN_OUT], wo_s=w_out[ATTN_OUT:ATTN_OUT + SSD_INNER], wo_p=w_out[ATTN_OUT + SSD_INNER:],
        wpq=p["peer_wq"][i].astype(BF16),
        keys=p["peer_keys"][i].reshape(PEER_HEADS * 2, N_KEYS, PEER_HALF).astype(BF16),
        u=p["peer_u"][i].astype(BF16), v=p["peer_v"][i].astype(BF16),
    )


def _block(n, pref):
    for c in pref:
        if n % c == 0:
            return c
    raise ValueError(f"no block size in {pref} divides {n}")


def _encoder(x, mods, weights, fg):
    B, L, _ = x.shape
    cos_t, sin_t = _rope_tables(L)
    tm = _block(L, (256, 128))
    tq = _block(L, (512, 256, 128))
    tk = _block(L // ATTN_SLOTS, (1024, 512, 256, 128))
    nchunk = _block(L // SSD_CHUNK, (4, 2, 1))
    lp = _block(L, (512, 256, 128))
    tb = _block(L, (1024, 512, 256, 128))
    tt = _block(B * L, (512, 256, 128))
    tg = _block(B * L, (64, 32, 16))
    peer = None
    for i in range(DEPTH):
        w, mod = weights[i], mods[i]
        if peer is None:
            q, k, v, z, xbc, pool_in, kd = _pre_mixer(x, mod, w, cos_t, sin_t, tm)
        else:
            q, k, v, z, xbc, pool_in, kd, x = _pre_mixer(x1, mod, w, cos_t, sin_t, tm, peer, mods[i - 1])
        attn = _attention(q, k, v, tq, tk)
        yf = _ssd_pass(xbc, kd, w, nchunk, reverse=False)
        ssd = _ssd_pass(xbc, kd, w, nchunk, reverse=True, yf=yf, z=z)
        pool = _pool(pool_in, w, lp)
        x1, h2, qp = _post_mixer(x, mod, attn, ssd, pool, w, tm)
        e, g = _peer_topk(qp.reshape(B * L, D_MODEL), w["keys"], tt)
        gates = _peer_gates(e, g, tg).reshape(B, L, N_EXPERTS)
        peer = _peer_dense(h2, w["u"], w["v"], gates, tb, 1024)
    return _final_norm(x1, peer, mods[DEPTH - 1], fg, tm)


def kernel(x_prompt, x_sample, c_prompt, c_sample, mod_w, mod_b, norm1_g, norm2_g, w_in, q_a_norm_g, w_q_b, kv_a_norm_g, w_kv_b, conv_w, conv_b, a_log_fwd, a_log_bwd, dt_bias_fwd, dt_bias_bwd, d_skip, ssd_norm_g, pool_w, pool_scale, w_out, peer_wq, peer_keys, peer_u, peer_v, final_norm_g):
    p = dict(mod_w=mod_w, mod_b=mod_b, norm1_g=norm1_g, norm2_g=norm2_g, w_in=w_in,
             q_a_norm_g=q_a_norm_g, w_q_b=w_q_b, kv_a_norm_g=kv_a_norm_g, w_kv_b=w_kv_b,
             conv_w=conv_w, conv_b=conv_b, a_log_fwd=a_log_fwd, a_log_bwd=a_log_bwd,
             dt_bias_fwd=dt_bias_fwd, dt_bias_bwd=dt_bias_bwd, d_skip=d_skip, ssd_norm_g=ssd_norm_g,
             pool_w=pool_w, pool_scale=pool_scale, w_out=w_out, peer_wq=peer_wq, peer_keys=peer_keys,
             peer_u=peer_u, peer_v=peer_v)
    weights = [_layer_weights(p, i) for i in range(DEPTH)]
    bp, bs = c_prompt.shape[0], c_sample.shape[0]
    c_pad = jnp.concatenate([c_prompt, c_sample, jnp.zeros((8 - bp - bs, D_MODEL), F32)], axis=0)
    mods = [_modulation(c_pad, mod_w[i], mod_b[i][None, :]) for i in range(DEPTH)]
    fg = final_norm_g[None, :]
    mods_p = [m[:bp].reshape(bp, 6, D_MODEL) for m in mods]
    mods_s = [m[bp:bp + bs].reshape(bs, 6, D_MODEL) for m in mods]
    return (_encoder(x_prompt, mods_p, weights, fg), _encoder(x_sample, mods_s, weights, fg))
```

```python
import functools
import math

import jax
import jax.numpy as jnp
from jax import lax
from jax.experimental import pallas as pl
from jax.experimental.pallas import tpu as pltpu

F32 = jnp.float32
BF16 = jnp.bfloat16
I32 = jnp.int32

D_MODEL = 2048
DEPTH = 2
EPS = 1e-6
ATTN_HEADS = 8
Q_LORA = 512
KV_LORA = 256
NOPE_DIM = 128
ROPE_DIM = 64
V_DIM = 128
ROPE_BASE = 10000.0
QK_PAD = 256
SSD_HEADS = 8
SSD_HEAD_DIM = 64
SSD_INNER = SSD_HEADS * SSD_HEAD_DIM
SSD_GROUPS = 2
SSD_STATE = 128
SSD_CHUNK = 128
D_CONV = 4
CONV_DIM = SSD_INNER + 2 * SSD_GROUPS * SSD_STATE
POOL_WINDOWS = (2, 4, 8, 16)
POOL_GROUP = 128
POOL_DIM = len(POOL_WINDOWS) * POOL_GROUP
ATTN_OUT = ATTN_HEADS * V_DIM
PEER_HEADS = 8
N_KEYS = 128
N_EXPERTS = N_KEYS * N_KEYS
PEER_HALF = 128
PEER_TOPK = 16
HK = PEER_HEADS * PEER_TOPK
HALO = 8
DT_LANE = 64

C_CQ = 0
C_CKV = C_CQ + Q_LORA
C_Z = C_CKV + KV_LORA
C_XBC = C_Z + SSD_INNER
C_POOL = C_XBC + CONV_DIM
C_KD = C_POOL + POOL_DIM
C_KDS = C_KD + 128
C_END = C_KDS + 128

VMEM_LIMIT = 56 * 1024 * 1024

_STAIR = [(a, PEER_TOPK // (a + 1)) for a in range(PEER_TOPK)]
_STAIR_ROWS = sum(n for _, n in _STAIR)
_STAIR_PAD = ((_STAIR_ROWS + 7) // 8) * 8


def _cparams(*sem):
    return pltpu.CompilerParams(dimension_semantics=sem, vmem_limit_bytes=VMEM_LIMIT)


def _const_spec(shape):
    nd = len(shape)
    return pl.BlockSpec(shape, lambda *_: (0,) * nd, pipeline_mode=pl.Buffered(1))


def _rms(x, g):
    return x * lax.rsqrt(jnp.mean(x * x, axis=-1, keepdims=True) + EPS) * g


def _silu(x):
    return x * jax.nn.sigmoid(x)


def _gelu(x):
    return 0.5 * x * (1.0 + lax.erf(x * (1.0 / math.sqrt(2.0))))


def _dot(a, b):
    return jnp.dot(a, b, preferred_element_type=F32)


def _dot_nt(a, b):
    return lax.dot_general(a, b, (((1,), (1,)), ((), ())), preferred_element_type=F32)


def _mod_kernel(c_ref, w_ref, b_ref, o_ref):
    cs = _silu(c_ref[...])
    o_ref[...] = _dot(cs.astype(BF16), w_ref[...].astype(BF16)) + b_ref[...]


def _modulation(c_pad, mod_w, mod_b):
    bn = 1024
    n = mod_w.shape[1]
    return pl.pallas_call(
        _mod_kernel,
        grid=(n // bn,),
        in_specs=[pl.BlockSpec((8, D_MODEL), lambda j: (0, 0)),
                  pl.BlockSpec((D_MODEL, bn), lambda j: (0, j)),
                  pl.BlockSpec((1, bn), lambda j: (0, j))],
        out_specs=pl.BlockSpec((8, bn), lambda j: (0, j)),
        out_shape=jax.ShapeDtypeStruct((8, n), F32),
        compiler_params=_cparams("parallel"),
        name="modulation",
    )(c_pad, mod_w, mod_b)


def _pre_kernel(*refs, residual):
    if residual:
        pe_ref, pmod_ref, x_ref, *refs = refs
    else:
        x_ref, *refs = refs
    (mod_ref, n1g_ref, wcat_ref, qg_ref, wq_ref, wqs_ref, kvg_ref, wk_ref, wv_ref, cos_ref, sin_ref,
     q_ref, k_ref, v_ref, z_ref, xbc_ref, pool_ref, kd_ref, *xo_ref) = refs
    x = x_ref[0]
    if residual:
        x = x + pmod_ref[0, 5:6, :] * pe_ref[0]
        xo_ref[0][0] = x
    sh1 = mod_ref[0, 0:1, :]
    sc1 = mod_ref[0, 1:2, :]
    h = _rms(x, n1g_ref[...]) * (1.0 + sc1) + sh1
    proj = _dot(h.astype(BF16), wcat_ref[...])
    z_ref[0] = proj[:, C_Z:C_XBC]
    xbc_ref[0] = proj[:, C_XBC:C_POOL]
    pool_ref[0] = proj[:, C_POOL:C_KD]
    kd = proj[:, C_KD:C_KDS]
    kd_ref[0] = kd
    cos = cos_ref[...]
    sin = sin_ref[...]
    scale = math.log2(math.e) / math.sqrt(NOPE_DIM + ROPE_DIM)

    cqn = _rms(proj[:, C_CQ:C_CKV], qg_ref[...]).astype(BF16)
    qm = _dot(cqn, wq_ref[...])
    qs = _dot(cqn, wqs_ref[...])
    for hd in range(ATTN_HEADS):
        o = hd * QK_PAD
        q_ref[0, hd, :, 0:NOPE_DIM] = (qm[:, o:o + NOPE_DIM] * scale).astype(BF16)
        rope = qm[:, o + NOPE_DIM:o + QK_PAD] * cos + qs[:, hd * 128:(hd + 1) * 128] * sin
        q_ref[0, hd, :, NOPE_DIM:QK_PAD] = (rope * scale).astype(BF16)

    ckvn = _rms(proj[:, C_CKV:C_Z], kvg_ref[...]).astype(BF16)
    kn = _dot(ckvn, wk_ref[...])
    vvt = _dot_nt(wv_ref[...], ckvn)
    krope = (kd * cos + proj[:, C_KDS:C_END] * sin).astype(BF16)
    for hd in range(ATTN_HEADS):
        k_ref[0, hd, :, 0:NOPE_DIM] = kn[:, hd * NOPE_DIM:(hd + 1) * NOPE_DIM].astype(BF16)
        k_ref[0, hd, :, NOPE_DIM:QK_PAD] = krope
        v_ref[0, hd] = vvt[hd * V_DIM:(hd + 1) * V_DIM, :].astype(BF16)


def _pre_mixer(x, mod, w, cos_t, sin_t, tm, peer=None, prev_mod=None):
    B, L, _ = x.shape
    grid = (B, L // tm)
    tok = lambda b, i: (b, i, 0)
    head = lambda b, i: (b, 0, i, 0)
    modspec = pl.BlockSpec((1, 6, D_MODEL), lambda b, i: (b, 0, 0))
    residual = peer is not None
    res_specs = [pl.BlockSpec((1, tm, D_MODEL), tok), modspec] if residual else []
    res_args = [peer, prev_mod] if residual else []
    res_out_specs = [pl.BlockSpec((1, tm, D_MODEL), tok)] if residual else []
    res_out_shape = [jax.ShapeDtypeStruct((B, L, D_MODEL), F32)] if residual else []
    return pl.pallas_call(
        functools.partial(_pre_kernel, residual=residual),
        grid=grid,
        in_specs=res_specs + [
                  pl.BlockSpec((1, tm, D_MODEL), tok),
                  modspec,
                  _const_spec((1, D_MODEL)),
                  _const_spec((D_MODEL, C_END)),
                  _const_spec((1, Q_LORA)),
                  _const_spec((Q_LORA, ATTN_HEADS * QK_PAD)),
                  _const_spec((Q_LORA, ATTN_HEADS * 128)),
                  _const_spec((1, KV_LORA)),
                  _const_spec((KV_LORA, ATTN_HEADS * NOPE_DIM)),
                  _const_spec((ATTN_HEADS * V_DIM, KV_LORA)),
                  pl.BlockSpec((tm, 128), lambda b, i: (i, 0)),
                  pl.BlockSpec((tm, 128), lambda b, i: (i, 0))],
        out_specs=[pl.BlockSpec((1, ATTN_HEADS, tm, QK_PAD), head),
                   pl.BlockSpec((1, ATTN_HEADS, tm, QK_PAD), head),
                   pl.BlockSpec((1, ATTN_HEADS, V_DIM, tm), lambda b, i: (b, 0, 0, i)),
                   pl.BlockSpec((1, tm, SSD_INNER), tok),
                   pl.BlockSpec((1, tm, CONV_DIM), tok),
                   pl.BlockSpec((1, tm, POOL_DIM), tok),
                   pl.BlockSpec((1, tm, 128), tok)] + res_out_specs,
        out_shape=[jax.ShapeDtypeStruct((B, ATTN_HEADS, L, QK_PAD), BF16),
                   jax.ShapeDtypeStruct((B, ATTN_HEADS, L, QK_PAD), BF16),
                   jax.ShapeDtypeStruct((B, ATTN_HEADS, V_DIM, L), BF16),
                   jax.ShapeDtypeStruct((B, L, SSD_INNER), F32),
                   jax.ShapeDtypeStruct((B, L, CONV_DIM), F32),
                   jax.ShapeDtypeStruct((B, L, POOL_DIM), F32),
                   jax.ShapeDtypeStruct((B, L, 128), F32)] + res_out_shape,
        compiler_params=_cparams("parallel", "parallel"),
        name="pre_mixer",
    )(*res_args, x, mod, w["n1g"], w["wcat"], w["qg"], w["wq"], w["wqs"], w["kvg"], w["wk"], w["wv"],
      cos_t, sin_t)


ATTN_SLOTS = 4


def _attn_kernel(q_ref, k_ref, vt_ref, o_ref, st_sc, p_sc, *, tk, nk):
    q = q_ref[0, 0]
    tq = q.shape[0]
    ntrip = nk // ATTN_SLOTS

    def chunk_start(j):
        return j * tk if isinstance(j, int) else pl.multiple_of(j * tk, tk)

    def scores(j):
        st = _dot_nt(k_ref[0, 0, pl.ds(chunk_start(j), tk), :], q)
        return st, jnp.max(st, axis=0, keepdims=True)

    def weighted_values(j, slot):
        return _dot(vt_ref[0, 0, :, pl.ds(chunk_start(j), tk)], p_sc[slot])

    def trip(jj, carry, first, last):
        for u in range(ATTN_SLOTS):
            j = ATTN_SLOTS * jj + u
            m_prev, l_prev, acc, a_prev, cmax, cmax_1 = carry
            if not (first and u == 0):
                acc = a_prev * acc + weighted_values(j - 1, (u - 1) % 2)
            if last and u >= ATTN_SLOTS - 2:
                cmax_2 = cmax_1
            else:
                st, cmax_2 = scores(j + 2)
                st_sc[(u + 2) % ATTN_SLOTS] = st
            m_new = jnp.maximum(m_prev, cmax)
            alpha = jnp.exp2(m_prev - m_new)
            p = jnp.exp2(st_sc[u] - m_new)
            l_new = alpha * l_prev + jnp.sum(p, axis=0, keepdims=True)
            p_sc[u % 2] = p.astype(BF16)
            carry = (m_new, l_new, acc, alpha, cmax_1, cmax_2)
        return carry

    st_sc[0], cmax0 = scores(0)
    st_sc[1], cmax1 = scores(1)
    carry = (jnp.full((1, tq), -jnp.inf, F32), jnp.zeros((1, tq), F32), jnp.zeros((V_DIM, tq), F32),
             jnp.ones((1, tq), F32), cmax0, cmax1)
    carry = trip(0, carry, True, ntrip == 1)
    for jj in range(1, ntrip - 1):
        carry = trip(jj, carry, False, False)
    if ntrip > 1:
        carry = trip(ntrip - 1, carry, False, True)
    _, l, acc, a_last, _, _ = carry
    acc = a_last * acc + weighted_values(nk - 1, (ATTN_SLOTS - 1) % 2)
    o_ref[0] = (acc / l).T.astype(o_ref.dtype)


def _attention(q, k, vt, tq, tk):
    B, H, L, _ = q.shape
    nk = L // tk
    assert nk % ATTN_SLOTS == 0, (L, tk)
    kern = functools.partial(_attn_kernel, tk=tk, nk=nk)
    return pl.pallas_call(
        kern,
        grid=(B, H, L // tq),
        in_specs=[pl.BlockSpec((1, 1, tq, QK_PAD), lambda b, h, i: (b, h, i, 0)),
                  pl.BlockSpec((1, 1, L, QK_PAD), lambda b, h, i: (b, h, 0, 0)),
                  pl.BlockSpec((1, 1, V_DIM, L), lambda b, h, i: (b, h, 0, 0))],
        out_specs=pl.BlockSpec((1, tq, V_DIM), lambda b, h, i: (b, i, h)),
        out_shape=jax.ShapeDtypeStruct((B, L, H * V_DIM), BF16),
        scratch_shapes=[pltpu.VMEM((ATTN_SLOTS, tk, tq), F32), pltpu.VMEM((2, tk, tq), BF16)],
        compiler_params=_cparams("parallel", "parallel", "arbitrary"),
        name="attention",
    )(q, k, vt)


def _halo_ext(prev_ref, cur_ref, next_ref, i, nblk):
    prev = jnp.where(i > 0, prev_ref[0], 0.0)
    nxt = jnp.where(i < nblk - 1, next_ref[0], 0.0)
    return jnp.concatenate([prev, cur_ref[0], nxt], axis=0)


def _shift_rows(ext, d, lb):
    n = ext.shape[0]
    r = ext if d == 0 else pltpu.roll(ext, (-d) % n, 0)
    return r[HALO:HALO + lb]


def _ssd_kernel(*refs, reverse, final, nblk, nchunk):
    if final:
        (xp_ref, xc_ref, xn_ref, kd_ref, cw_ref, cb_ref, dtb_ref, a_ref, yf_ref, z_ref, dsk_ref, ng_ref,
         o_ref, st_sc) = refs
    else:
        xp_ref, xc_ref, xn_ref, kd_ref, cw_ref, cb_ref, dtb_ref, a_ref, o_ref, st_sc = refs
    step = pl.program_id(1)
    blk = (nblk - 1 - step) if reverse else step
    lb = nchunk * SSD_CHUNK

    @pl.when(step == 0)
    def _():
        st_sc[...] = jnp.zeros(st_sc.shape, F32)

    ext = _halo_ext(xp_ref, xc_ref, xn_ref, blk, nblk)
    conv = cb_ref[...] + sum(cw_ref[kk:kk + 1, :] * _shift_rows(ext, kk - D_CONV // 2, lb) for kk in range(D_CONV))
    act = _silu(conv)
    dt_all = jax.nn.softplus(kd_ref[0] + dtb_ref[...])
    da_all = dt_all * a_ref[...]

    row = lax.broadcasted_iota(I32, (SSD_CHUNK, SSD_CHUNK), 0)
    col = lax.broadcasted_iota(I32, (SSD_CHUNK, SSD_CHUNK), 1)
    keep = (row <= col) if reverse else (row >= col)
    tri = jnp.where(keep, 1.0, 0.0).astype(F32)
    lane = lax.broadcasted_iota(I32, (SSD_CHUNK, 128), 1)
    low = lane < SSD_HEAD_DIM
    dlane = DT_LANE + (SSD_HEADS if reverse else 0)
    edge = 0 if reverse else SSD_CHUNK - 1

    def pair(arr, h0):
        return jnp.where(low, arr[:, dlane + h0:dlane + h0 + 1], arr[:, dlane + h0 + 1:dlane + h0 + 2])

    ys = []
    for cc in (range(nchunk - 1, -1, -1) if reverse else range(nchunk)):
        r0 = cc * SSD_CHUNK
        da = da_all[r0:r0 + SSD_CHUNK]
        dt = dt_all[r0:r0 + SSD_CHUNK]
        acum = jnp.dot(tri, da, preferred_element_type=F32, precision=lax.Precision.HIGHEST)
        acum_t = acum.T
        a_edge = acum[edge:edge + 1, :]
        ycols = [None] * (SSD_HEADS // 2)
        for g in range(SSD_GROUPS):
            bg = act[r0:r0 + SSD_CHUNK, SSD_INNER + g * SSD_STATE:SSD_INNER + (g + 1) * SSD_STATE]
            cg = act[r0:r0 + SSD_CHUNK, SSD_INNER + (SSD_GROUPS + g) * SSD_STATE:
                     SSD_INNER + (SSD_GROUPS + g + 1) * SSD_STATE]
            cgb = cg.astype(BF16)
            cb = _dot_nt(cgb, bg.astype(BF16))
            bgt = bg.T.astype(BF16)
            for pp in range(SSD_HEADS // SSD_GROUPS // 2):
                pi = g * 2 + pp
                h0 = 2 * pi
                xdt = act[r0:r0 + SSD_CHUNK, pi * 128:(pi + 1) * 128] * pair(dt, h0)
                xdtb = xdt.astype(BF16)
                yd = []
                for hh in (h0, h0 + 1):
                    seg = acum[:, dlane + hh:dlane + hh + 1] - acum_t[dlane + hh:dlane + hh + 1, :]
                    decay = jnp.exp(jnp.where(keep, seg, -jnp.inf))
                    yd.append(_dot((cb * decay).astype(BF16), xdtb))
                y_diag = jnp.where(low, yd[0], yd[1])
                st = st_sc[pi]
                y_off = _dot(cgb, st.astype(BF16)) * jnp.exp(pair(acum, h0))
                ycols[pi] = y_diag + y_off
                to_edge = jnp.exp(pair(a_edge - acum, h0))
                st_sc[pi] = st * jnp.exp(pair(a_edge, h0)) + _dot(bgt, (xdt * to_edge).astype(BF16))
        ys.append((cc, jnp.concatenate(ycols, axis=1)))
    y = jnp.concatenate([v for _, v in sorted(ys, key=lambda t: t[0])], axis=0)

    if final:
        tot = yf_ref[0] + y + act[:, 0:SSD_INNER] * dsk_ref[...]
        o_ref[0] = _rms(tot * _silu(z_ref[0]), ng_ref[...]).astype(o_ref.dtype)
    else:
        o_ref[0] = y


def _ssd_pass(xbc, kd, w, nchunk, reverse, yf=None, z=None):
    B, L, _ = xbc.shape
    lb = nchunk * SSD_CHUNK
    nblk = L // lb
    hb = lb // HALO
    final = yf is not None
    bi = (lambda i: nblk - 1 - i) if reverse else (lambda i: i)
    cur = lambda b, i: (b, bi(i), 0)
    prv = lambda b, i: (b, jnp.maximum(bi(i) * hb - 1, 0), 0)
    nxt = lambda b, i: (b, jnp.minimum((bi(i) + 1) * hb, L // HALO - 1), 0)
    in_specs = [pl.BlockSpec((1, HALO, CONV_DIM), prv),
                pl.BlockSpec((1, lb, CONV_DIM), cur),
                pl.BlockSpec((1, HALO, CONV_DIM), nxt),
                pl.BlockSpec((1, lb, 128), cur),
                _const_spec((D_CONV, CONV_DIM)), _const_spec((1, CONV_DIM)),
                _const_spec((1, 128)), _const_spec((1, 128))]
    args = [xbc, xbc, xbc, kd, w["conv_w"], w["conv_b"], w["dt_bias"], w["a_neg"]]
    if final:
        in_specs += [pl.BlockSpec((1, lb, SSD_INNER), cur), pl.BlockSpec((1, lb, SSD_INNER), cur),
                     _const_spec((1, SSD_INNER)), _const_spec((1, SSD_INNER))]
        args += [yf, z, w["d_skip"], w["ssd_g"]]
    kern = functools.partial(_ssd_kernel, reverse=reverse, final=final, nblk=nblk, nchunk=nchunk)
    return pl.pallas_call(
        kern,
        grid=(B, nblk),
        in_specs=in_specs,
        out_specs=pl.BlockSpec((1, lb, SSD_INNER), cur),
        out_shape=jax.ShapeDtypeStruct((B, L, SSD_INNER), BF16 if final else F32),
        scratch_shapes=[pltpu.VMEM((SSD_HEADS // 2, SSD_STATE, 128), F32)],
        compiler_params=_cparams("parallel", "arbitrary"),
        name="ssd_bwd" if reverse else "ssd_fwd",
    )(*args)


def _pool_kernel(up_ref, uc_ref, un_ref, pw_ref, ps_ref, o_ref, *, nblk, lb, seq):
    blk = pl.program_id(1)
    ext = _halo_ext(up_ref, uc_ref, un_ref, blk, nblk)
    n = ext.shape[0]
    t = blk * lb + lax.broadcasted_iota(I32, (lb, 1), 0)
    for gi, wdw in enumerate(POOL_WINDOWS):
        e = ext[:, gi * POOL_GROUP:(gi + 1) * POOL_GROUP]
        run, width = e, 1
        while width < wdw:
            run = run + pltpu.roll(run, width, 0)
            width *= 2
        tot = _shift_rows(run, wdw // 2 - 1, lb)
        cnt = jnp.minimum(t + wdw // 2, seq) - jnp.maximum(t - wdw // 2, 0)
        d = tot / cnt.astype(F32) - e[HALO:HALO + lb]
        y = _dot(d.astype(BF16), pw_ref[gi])
        o_ref[0, :, gi * POOL_GROUP:(gi + 1) * POOL_GROUP] = (
            y * ps_ref[:, gi * POOL_GROUP:(gi + 1) * POOL_GROUP]).astype(o_ref.dtype)


def _pool(u, w, lb):
    B, L, _ = u.shape
    nblk = L // lb
    hb = lb // HALO
    cur = lambda b, i: (b, i, 0)
    prv = lambda b, i: (b, jnp.maximum(i * hb - 1, 0), 0)
    nxt = lambda b, i: (b, jnp.minimum((i + 1) * hb, L // HALO - 1), 0)
    kern = functools.partial(_pool_kernel, nblk=nblk, lb=lb, seq=L)
    return pl.pallas_call(
        kern,
        grid=(B, nblk),
        in_specs=[pl.BlockSpec((1, HALO, POOL_DIM), prv),
                  pl.BlockSpec((1, lb, POOL_DIM), cur),
                  pl.BlockSpec((1, HALO, POOL_DIM), nxt),
                  _const_spec((len(POOL_WINDOWS), POOL_GROUP, POOL_GROUP)),
                  _const_spec((1, POOL_DIM))],
        out_specs=pl.BlockSpec((1, lb, POOL_DIM), cur),
        out_shape=jax.ShapeDtypeStruct((B, L, POOL_DIM), BF16),
        compiler_params=_cparams("parallel", "parallel"),
        name="pool",
    )(u, u, u, w["pool_w"], w["pool_scale"])


def _post_kernel(x_ref, mod_ref, at_ref, sd_ref, po_ref, woa_ref, wos_ref, wop_ref, n2g_ref, wpq_ref,
                 x1_ref, h2_ref, qp_ref):
    g1 = mod_ref[0, 2:3, :]
    sh2 = mod_ref[0, 3:4, :]
    sc2 = mod_ref[0, 4:5, :]
    mix = _dot(at_ref[0], woa_ref[...]) + _dot(sd_ref[0], wos_ref[...]) + _dot(po_ref[0], wop_ref[...])
    x1 = x_ref[0] + g1 * mix
    x1_ref[0] = x1
    h2 = (_rms(x1, n2g_ref[...]) * (1.0 + sc2) + sh2).astype(BF16)
    h2_ref[0] = h2
    qp_ref[0] = _dot(h2, wpq_ref[...]).astype(BF16)


def _post_mixer(x, mod, attn, ssd, pool, w, tm):
    B, L, _ = x.shape
    tok = lambda b, i: (b, i, 0)
    return pl.pallas_call(
        _post_kernel,
        grid=(B, L // tm),
        in_specs=[pl.BlockSpec((1, tm, D_MODEL), tok),
                  pl.BlockSpec((1, 6, D_MODEL), lambda b, i: (b, 0, 0)),
                  pl.BlockSpec((1, tm, ATTN_OUT), tok),
                  pl.BlockSpec((1, tm, SSD_INNER), tok),
                  pl.BlockSpec((1, tm, POOL_DIM), tok),
                  _const_spec((ATTN_OUT, D_MODEL)), _const_spec((SSD_INNER, D_MODEL)),
                  _const_spec((POOL_DIM, D_MODEL)), _const_spec((1, D_MODEL)),
                  _const_spec((D_MODEL, D_MODEL))],
        out_specs=[pl.BlockSpec((1, tm, D_MODEL), tok)] * 3,
        out_shape=[jax.ShapeDtypeStruct((B, L, D_MODEL), F32),
                   jax.ShapeDtypeStruct((B, L, D_MODEL), BF16),
                   jax.ShapeDtypeStruct((B, L, D_MODEL), BF16)],
        compiler_params=_cparams("parallel", "parallel"),
        name="post_mixer",
    )(x, mod, attn, ssd, pool, w["wo_a"], w["wo_s"], w["wo_p"], w["n2g"], w["wpq"])


def _top16(s, out_v, out_i):
    n = s.shape[0]
    rows = lax.broadcasted_iota(I32, s.shape, 0).astype(F32)
    for r in range(PEER_TOPK):
        m = jnp.max(s, axis=0, keepdims=True)
        idx = jnp.min(jnp.where(s == m, rows, float(n)), axis=0, keepdims=True)
        out_v[r:r + 1, :] = m
        out_i[r:r + 1, :] = idx
        s = jnp.where(rows == idx, -jnp.inf, s)


def _topk_kernel(qp_ref, keys_ref, e_ref, g_ref, v1_sc, i1_sc, v2_sc, i2_sc, cv_sc, ci_sc, tv_sc, ti_sc,
                 eo_sc, go_sc):
    cv_sc[...] = jnp.full(cv_sc.shape, -jnp.inf, F32)
    ci_sc[...] = jnp.zeros(ci_sc.shape, F32)
    for hd in range(PEER_HEADS):
        for half, (vs, is_) in enumerate(((v1_sc, i1_sc), (v2_sc, i2_sc))):
            c0 = (hd * 2 + half) * PEER_HALF
            s = _dot_nt(keys_ref[hd * 2 + half], qp_ref[:, c0:c0 + PEER_HALF])
            _top16(s, vs, is_)
        off = 0
        for a, nb in _STAIR:
            cv_sc[off:off + nb, :] = v1_sc[a:a + 1, :] + v2_sc[0:nb, :]
            ci_sc[off:off + nb, :] = i1_sc[a:a + 1, :] * float(N_KEYS) + i2_sc[0:nb, :]
            off += nb
        cand = cv_sc[...]
        cidx = ci_sc[...]
        rows = lax.broadcasted_iota(I32, cand.shape, 0).astype(F32)
        for r in range(PEER_TOPK):
            m = jnp.max(cand, axis=0, keepdims=True)
            pos = jnp.min(jnp.where(cand == m, rows, float(_STAIR_PAD)), axis=0, keepdims=True)
            hit = rows == pos
            tv_sc[r:r + 1, :] = m
            ti_sc[r:r + 1, :] = jnp.sum(jnp.where(hit, cidx, 0.0), axis=0, keepdims=True)
            cand = jnp.where(hit, -jnp.inf, cand)
        tv = tv_sc[...]
        p = jnp.exp(tv - tv[0:1, :])
        go_sc[hd * PEER_TOPK:(hd + 1) * PEER_TOPK, :] = p / jnp.sum(p, axis=0, keepdims=True)
        eo_sc[hd * PEER_TOPK:(hd + 1) * PEER_TOPK, :] = ti_sc[...]
    e_ref[...] = eo_sc[...].T.astype(I32)
    g_ref[...] = go_sc[...].T


def _peer_topk(qp, keys, tm):
    T = qp.shape[0]
    return pl.pallas_call(
        _topk_kernel,
        grid=(T // tm,),
        in_specs=[pl.BlockSpec((tm, D_MODEL), lambda i: (i, 0)),
                  _const_spec((PEER_HEADS * 2, N_KEYS, PEER_HALF))],
        out_specs=[pl.BlockSpec((tm, HK), lambda i: (i, 0))] * 2,
        out_shape=[jax.ShapeDtypeStruct((T, HK), I32), jax.ShapeDtypeStruct((T, HK), F32)],
        scratch_shapes=[pltpu.VMEM((PEER_TOPK, tm), F32)] * 4
                       + [pltpu.VMEM((_STAIR_PAD, tm), F32)] * 2
                       + [pltpu.VMEM((PEER_TOPK, tm), F32)] * 2
                       + [pltpu.VMEM((HK, tm), F32)] * 2,
        compiler_params=_cparams("parallel"),
        name="peer_topk",
    )(qp, keys)


G_PITCH = N_KEYS + 8


G_GROUP = 16


def _gate_kernel(e_ref, g_ref, o_ref, gs_sc, *, tg):
    sub = lax.broadcasted_iota(I32, (N_KEYS, HK), 0)
    ngroup = tg // G_GROUP

    def build(n, slot):
        for u in range(G_GROUP):
            t = n * G_GROUP + u
            er = e_ref[pl.ds(t, 1), :]
            gr = g_ref[pl.ds(t, 1), :]
            i1 = lax.shift_right_logical(er, 7)
            i2 = lax.bitwise_and(er, N_KEYS - 1)
            a = jnp.where(sub == i1, 1.0, 0.0).astype(BF16)
            b = jnp.where(sub == i2, gr, 0.0).astype(BF16)
            gs_sc[slot, pl.ds(u * G_PITCH, N_KEYS), :] = _dot_nt(a, b)

    def regroup(n, slot):
        row = n * G_GROUP if isinstance(n, int) else pl.multiple_of(n * G_GROUP, G_GROUP)
        for c in range(N_KEYS):
            lo = gs_sc[slot, pl.ds(c, 8, stride=G_PITCH), :]
            hi = gs_sc[slot, pl.ds(8 * G_PITCH + c, 8, stride=G_PITCH), :]
            o_ref[pl.ds(row, G_GROUP), c * N_KEYS:(c + 1) * N_KEYS] = (
                jnp.concatenate([lo, hi], axis=0).astype(o_ref.dtype))

    def body(nn, carry):
        n = 2 * nn + 1
        build(n, 1)
        regroup(n - 1, 0)
        build(n + 1, 0)
        regroup(n, 1)
        return carry

    build(0, 0)
    lax.fori_loop(0, (ngroup - 1) // 2, body, 0)
    build(ngroup - 1, 1)
    regroup(ngroup - 2, 0)
    regroup(ngroup - 1, 1)


def _peer_gates(e, g, tg):
    T = e.shape[0]
    assert tg % (2 * G_GROUP) == 0, tg
    kern = functools.partial(_gate_kernel, tg=tg)
    return pl.pallas_call(
        kern,
        grid=(T // tg,),
        in_specs=[pl.BlockSpec((tg, HK), lambda i: (i, 0))] * 2,
        out_specs=pl.BlockSpec((tg, N_EXPERTS), lambda i: (i, 0)),
        out_shape=jax.ShapeDtypeStruct((T, N_EXPERTS), BF16),
        scratch_shapes=[pltpu.VMEM((2, G_GROUP * G_PITCH, N_KEYS), F32)],
        compiler_params=_cparams("parallel"),
        name="peer_gates",
    )(e, g)


def _dense_kernel(h_ref, u_ref, v_ref, gt_ref, o_ref):
    @pl.when(pl.program_id(2) == 0)
    def _():
        o_ref[...] = jnp.zeros(o_ref.shape, F32)

    a = _gelu(_dot_nt(h_ref[0], u_ref[...]))
    wgt = (a * gt_ref[0].astype(F32)).astype(BF16)
    o_ref[0] += _dot(wgt, v_ref[...])


def _peer_dense(h2, u, v, gates, tb, eb):
    B, L, _ = h2.shape
    tok = lambda b, i, j: (b, i, 0)
    return pl.pallas_call(
        _dense_kernel,
        grid=(B, L // tb, N_EXPERTS // eb),
        in_specs=[pl.BlockSpec((1, tb, D_MODEL), tok),
                  pl.BlockSpec((eb, D_MODEL), lambda b, i, j: (j, 0)),
                  pl.BlockSpec((eb, D_MODEL), lambda b, i, j: (j, 0)),
                  pl.BlockSpec((1, tb, eb), lambda b, i, j: (b, i, j))],
        out_specs=pl.BlockSpec((1, tb, D_MODEL), tok),
        out_shape=jax.ShapeDtypeStruct((B, L, D_MODEL), F32),
        compiler_params=_cparams("parallel", "parallel", "arbitrary"),
        name="peer_dense",
    )(h2, u, v, gates)


def _final_kernel(x1_ref, pe_ref, mod_ref, fg_ref, o_ref):
    o_ref[0] = _rms(x1_ref[0] + mod_ref[0, 5:6, :] * pe_ref[0], fg_ref[...])


def _final_norm(x1, peer, mod, fg, tm):
    B, L, _ = x1.shape
    tok = lambda b, i: (b, i, 0)
    return pl.pallas_call(
        _final_kernel,
        grid=(B, L // tm),
        in_specs=[pl.BlockSpec((1, tm, D_MODEL), tok), pl.BlockSpec((1, tm, D_MODEL), tok),
                  pl.BlockSpec((1, 6, D_MODEL), lambda b, i: (b, 0, 0)), _const_spec((1, D_MODEL))],
        out_specs=pl.BlockSpec((1, tm, D_MODEL), tok),
        out_shape=jax.ShapeDtypeStruct((B, L, D_MODEL), F32),
        compiler_params=_cparams("parallel", "parallel"),
        name="final_norm",
    )(x1, peer, mod, fg)


def _rope_tables(L):
    inv = 1.0 / (ROPE_BASE ** (jnp.arange(0, ROPE_DIM, 2, dtype=F32) / ROPE_DIM))
    ang = jnp.arange(L, dtype=F32)[:, None] * inv[None, :]
    cos, sin = jnp.cos(ang), jnp.sin(ang)
    zero = jnp.zeros((L, 128 - ROPE_DIM), F32)
    return jnp.concatenate([cos, cos, zero], axis=1), jnp.concatenate([-sin, sin, zero], axis=1)


def _swap_halves(w):
    half = w.shape[-1] // 2
    return jnp.concatenate([w[..., half:], w[..., :half]], axis=-1)


def _layer_weights(p, i):
    w_in = p["w_in"][i]
    s0, s1, s2, s3, s4, s5 = (Q_LORA, Q_LORA + KV_LORA, Q_LORA + KV_LORA + ROPE_DIM,
                              Q_LORA + KV_LORA + ROPE_DIM + SSD_INNER,
                              Q_LORA + KV_LORA + ROPE_DIM + SSD_INNER + CONV_DIM,
                              Q_LORA + KV_LORA + ROPE_DIM + SSD_INNER + CONV_DIM + 2 * SSD_HEADS)
    w_kr = w_in[:, s1:s2]
    zcol = lambda n: jnp.zeros((D_MODEL, n), F32)
    wcat = jnp.concatenate(
        [w_in[:, :s0], w_in[:, s0:s1], w_in[:, s2:s3], w_in[:, s3:s4], w_in[:, s5:],
         w_kr, w_in[:, s4:s5], zcol(128 - ROPE_DIM - 2 * SSD_HEADS),
         _swap_halves(w_kr), zcol(128 - ROPE_DIM)], axis=1).astype(BF16)
    wqb = p["w_q_b"][i].reshape(Q_LORA, ATTN_HEADS, NOPE_DIM + ROPE_DIM)
    zq = jnp.zeros((Q_LORA, ATTN_HEADS, 128 - ROPE_DIM), F32)
    wq = jnp.concatenate([wqb, zq], axis=2).reshape(Q_LORA, ATTN_HEADS * QK_PAD).astype(BF16)
    wqs = jnp.concatenate([_swap_halves(wqb[:, :, NOPE_DIM:]), zq], axis=2).reshape(
        Q_LORA, ATTN_HEADS * 128).astype(BF16)
    wkv = p["w_kv_b"][i].reshape(KV_LORA, ATTN_HEADS, NOPE_DIM + V_DIM)
    wk = wkv[:, :, :NOPE_DIM].reshape(KV_LORA, ATTN_HEADS * NOPE_DIM).astype(BF16)
    wv = wkv[:, :, NOPE_DIM:].reshape(KV_LORA, ATTN_HEADS * V_DIM).T.astype(BF16)
    lane_pad = lambda f, b: jnp.concatenate(
        [jnp.zeros((DT_LANE,), F32), f, b, jnp.zeros((128 - DT_LANE - 2 * SSD_HEADS,), F32)])[None, :]
    w_out = p["w_out"][i].astype(BF16)
    return dict(
        n1g=p["norm1_g"][i][None, :], n2g=p["norm2_g"][i][None, :],
        wcat=wcat, qg=p["q_a_norm_g"][i][None, :], wq=wq, wqs=wqs,
        kvg=p["kv_a_norm_g"][i][None, :], wk=wk, wv=wv,
        conv_w=p["conv_w"][i], conv_b=p["conv_b"][i][None, :],
        dt_bias=lane_pad(p["dt_bias_fwd"][i], p["dt_bias_bwd"][i]),
        a_neg=lane_pad(-jnp.exp(p["a_log_fwd"][i]), -jnp.exp(p["a_log_bwd"][i])),
        d_skip=jnp.repeat(p["d_skip"][i], SSD_HEAD_DIM)[None, :], ssd_g=p["ssd_norm_g"][i][None, :],
        pool_w=p["pool_w"][i].astype(BF16), pool_scale=p["pool_scale"][i][None, :],
        wo_a=w_out[:ATTN_OUT], wo_s=w_out[ATTN_OUT:ATTN_OUT + SSD_INNER], wo_p=w_out[ATTN_OUT + SSD_INNER:],
        wpq=p["peer_wq"][i].astype(BF16),
        keys=p["peer_keys"][i].reshape(PEER_HEADS * 2, N_KEYS, PEER_HALF).astype(BF16),
        u=p["peer_u"][i].astype(BF16), v=p["peer_v"][i].astype(BF16),
    )


def _block(n, pref):
    for c in pref:
        if n % c == 0:
            return c
    raise ValueError(f"no block size in {pref} divides {n}")


def _encoder(x, mods, weights, fg):
    B, L, _ = x.shape
    cos_t, sin_t = _rope_tables(L)
    tm = _block(L, (256, 128))
    tq = _block(L, (512, 256, 128))
    tk = _block(L // ATTN_SLOTS, (1024, 512, 256, 128))
    nchunk = _block(L // SSD_CHUNK, (4, 2, 1))
    lp = _block(L, (512, 256, 128))
    tb = _block(L, (1024, 512, 256, 128))
    tt = _block(B * L, (256, 128))
    tg = _block(B * L, (64, 32))
    peer = None
    for i in range(DEPTH):
        w, mod = weights[i], mods[i]
        if peer is None:
            q, k, v, z, xbc, pool_in, kd = _pre_mixer(x, mod, w, cos_t, sin_t, tm)
        else:
            q, k, v, z, xbc, pool_in, kd, x = _pre_mixer(x1, mod, w, cos_t, sin_t, tm, peer, mods[i - 1])
        attn = _attention(q, k, v, tq, tk)
        yf = _ssd_pass(xbc, kd, w, nchunk, reverse=False)
        ssd = _ssd_pass(xbc, kd, w, nchunk, reverse=True, yf=yf, z=z)
        pool = _pool(pool_in, w, lp)
        x1, h2, qp = _post_mixer(x, mod, attn, ssd, pool, w, tm)
        e, g = _peer_topk(qp.reshape(B * L, D_MODEL), w["keys"], tt)
        gates = _peer_gates(e, g, tg).reshape(B, L, N_EXPERTS)
        peer = _peer_dense(h2, w["u"], w["v"], gates, tb, 1024)
    return _final_norm(x1, peer, mods[DEPTH - 1], fg, tm)


def kernel(x_prompt, x_sample, c_prompt, c_sample, mod_w, mod_b, norm1_g, norm2_g, w_in, q_a_norm_g, w_q_b, kv_a_norm_g, w_kv_b, conv_w, conv_b, a_log_fwd, a_log_bwd, dt_bias_fwd, dt_bias_bwd, d_skip, ssd_norm_g, pool_w, pool_scale, w_out, peer_wq, peer_keys, peer_u, peer_v, final_norm_g):
    p = dict(mod_w=mod_w, mod_b=mod_b, norm1_g=norm1_g, norm2_g=norm2_g, w_in=w_in,
             q_a_norm_g=q_a_norm_g, w_q_b=w_q_b, kv_a_norm_g=kv_a_norm_g, w_kv_b=w_kv_b,
             conv_w=conv_w, conv_b=conv_b, a_log_fwd=a_log_fwd, a_log_bwd=a_log_bwd,
             dt_bias_fwd=dt_bias_fwd, dt_bias_bwd=dt_bias_bwd, d_skip=d_skip, ssd_norm_g=ssd_norm_g,
             pool_w=pool_w, pool_scale=pool_scale, w_out=w_out, peer_wq=peer_wq, peer_keys=peer_keys,
             peer_u=peer_u, peer_v=peer_v)
    weights = [_layer_weights(p, i) for i in range(DEPTH)]
    bp, bs = c_prompt.shape[0], c_sample.shape[0]
    c_pad = jnp.concatenate([c_prompt, c_sample, jnp.zeros((8 - bp - bs, D_MODEL), F32)], axis=0)
    mods = [_modulation(c_pad, mod_w[i], mod_b[i][None, :]) for i in range(DEPTH)]
    fg = final_norm_g[None, :]
    mods_p = [m[:bp].reshape(bp, 6, D_MODEL) for m in mods]
    mods_s = [m[bp:bp + bs].reshape(bs, 6, D_MODEL) for m in mods]
    return (_encoder(x_prompt, mods_p, weights, fg), _encoder(x_sample, mods_s, weights, fg))
```

```python
import functools
import math

import jax
import jax.numpy as jnp
from jax import lax
from jax.experimental import pallas as pl
from jax.experimental.pallas import tpu as pltpu

F32 = jnp.float32
BF16 = jnp.bfloat16
I32 = jnp.int32

D_MODEL = 2048
DEPTH = 2
EPS = 1e-6
ATTN_HEADS = 8
Q_LORA = 512
KV_LORA = 256
NOPE_DIM = 128
ROPE_DIM = 64
V_DIM = 128
ROPE_BASE = 10000.0
QK_PAD = 256
SSD_HEADS = 8
SSD_HEAD_DIM = 64
SSD_INNER = SSD_HEADS * SSD_HEAD_DIM
SSD_GROUPS = 2
SSD_STATE = 128
SSD_CHUNK = 128
D_CONV = 4
CONV_DIM = SSD_INNER + 2 * SSD_GROUPS * SSD_STATE
POOL_WINDOWS = (2, 4, 8, 16)
POOL_GROUP = 128
POOL_DIM = len(POOL_WINDOWS) * POOL_GROUP
ATTN_OUT = ATTN_HEADS * V_DIM
PEER_HEADS = 8
N_KEYS = 128
N_EXPERTS = N_KEYS * N_KEYS
PEER_HALF = 128
PEER_TOPK = 16
HK = PEER_HEADS * PEER_TOPK
HALO = 8
DT_LANE = 64

C_CQ = 0
C_CKV = C_CQ + Q_LORA
C_Z = C_CKV + KV_LORA
C_XBC = C_Z + SSD_INNER
C_POOL = C_XBC + CONV_DIM
C_KD = C_POOL + POOL_DIM
C_KDS = C_KD + 128
C_END = C_KDS + 128

VMEM_LIMIT = 56 * 1024 * 1024

_STAIR = [(a, PEER_TOPK // (a + 1)) for a in range(PEER_TOPK)]
_STAIR_ROWS = sum(n for _, n in _STAIR)
_STAIR_PAD = ((_STAIR_ROWS + 7) // 8) * 8


def _cparams(*sem):
    return pltpu.CompilerParams(dimension_semantics=sem, vmem_limit_bytes=VMEM_LIMIT)


def _const_spec(shape):
    nd = len(shape)
    return pl.BlockSpec(shape, lambda *_: (0,) * nd, pipeline_mode=pl.Buffered(1))


def _rms(x, g):
    return x * lax.rsqrt(jnp.mean(x * x, axis=-1, keepdims=True) + EPS) * g


def _silu(x):
    return x * jax.nn.sigmoid(x)


def _gelu(x):
    return 0.5 * x * (1.0 + lax.erf(x * (1.0 / math.sqrt(2.0))))


def _dot(a, b):
    return jnp.dot(a, b, preferred_element_type=F32)


def _dot_nt(a, b):
    return lax.dot_general(a, b, (((1,), (1,)), ((), ())), preferred_element_type=F32)


def _mod_kernel(c_ref, w_ref, b_ref, o_ref):
    cs = _silu(c_ref[...])
    o_ref[...] = _dot(cs.astype(BF16), w_ref[...].astype(BF16)) + b_ref[...]


def _modulation(c_pad, mod_w, mod_b):
    bn = 1024
    n = mod_w.shape[1]
    return pl.pallas_call(
        _mod_kernel,
        grid=(n // bn,),
        in_specs=[pl.BlockSpec((8, D_MODEL), lambda j: (0, 0)),
                  pl.BlockSpec((D_MODEL, bn), lambda j: (0, j)),
                  pl.BlockSpec((1, bn), lambda j: (0, j))],
        out_specs=pl.BlockSpec((8, bn), lambda j: (0, j)),
        out_shape=jax.ShapeDtypeStruct((8, n), F32),
        compiler_params=_cparams("parallel"),
        name="modulation",
    )(c_pad, mod_w, mod_b)


def _pre_kernel(*refs, residual):
    if residual:
        pe_ref, pmod_ref, x_ref, *refs = refs
    else:
        x_ref, *refs = refs
    (mod_ref, n1g_ref, wcat_ref, qg_ref, wq_ref, wqs_ref, kvg_ref, wk_ref, wv_ref, cos_ref, sin_ref,
     q_ref, k_ref, v_ref, z_ref, xbc_ref, pool_ref, kd_ref, *xo_ref) = refs
    x = x_ref[0]
    if residual:
        x = x + pmod_ref[0, 5:6, :] * pe_ref[0]
        xo_ref[0][0] = x
    sh1 = mod_ref[0, 0:1, :]
    sc1 = mod_ref[0, 1:2, :]
    h = _rms(x, n1g_ref[...]) * (1.0 + sc1) + sh1
    proj = _dot(h.astype(BF16), wcat_ref[...])
    z_ref[0] = proj[:, C_Z:C_XBC]
    xbc_ref[0] = proj[:, C_XBC:C_POOL]
    pool_ref[0] = proj[:, C_POOL:C_KD]
    kd = proj[:, C_KD:C_KDS]
    kd_ref[0] = kd
    cos = cos_ref[...]
    sin = sin_ref[...]
    scale = math.log2(math.e) / math.sqrt(NOPE_DIM + ROPE_DIM)

    cqn = _rms(proj[:, C_CQ:C_CKV], qg_ref[...]).astype(BF16)
    qm = _dot(cqn, wq_ref[...])
    qs = _dot(cqn, wqs_ref[...])
    for hd in range(ATTN_HEADS):
        o = hd * QK_PAD
        q_ref[0, hd, :, 0:NOPE_DIM] = (qm[:, o:o + NOPE_DIM] * scale).astype(BF16)
        rope = qm[:, o + NOPE_DIM:o + QK_PAD] * cos + qs[:, hd * 128:(hd + 1) * 128] * sin
        q_ref[0, hd, :, NOPE_DIM:QK_PAD] = (rope * scale).astype(BF16)

    ckvn = _rms(proj[:, C_CKV:C_Z], kvg_ref[...]).astype(BF16)
    kn = _dot(ckvn, wk_ref[...])
    vvt = _dot_nt(wv_ref[...], ckvn)
    krope = (kd * cos + proj[:, C_KDS:C_END] * sin).astype(BF16)
    for hd in range(ATTN_HEADS):
        k_ref[0, hd, :, 0:NOPE_DIM] = kn[:, hd * NOPE_DIM:(hd + 1) * NOPE_DIM].astype(BF16)
        k_ref[0, hd, :, NOPE_DIM:QK_PAD] = krope
        v_ref[0, hd] = vvt[hd * V_DIM:(hd + 1) * V_DIM, :].astype(BF16)


def _pre_mixer(x, mod, w, cos_t, sin_t, tm, peer=None, prev_mod=None):
    B, L, _ = x.shape
    grid = (B, L // tm)
    tok = lambda b, i: (b, i, 0)
    head = lambda b, i: (b, 0, i, 0)
    modspec = pl.BlockSpec((1, 6, D_MODEL), lambda b, i: (b, 0, 0))
    residual = peer is not None
    res_specs = [pl.BlockSpec((1, tm, D_MODEL), tok), modspec] if residual else []
    res_args = [peer, prev_mod] if residual else []
    res_out_specs = [pl.BlockSpec((1, tm, D_MODEL), tok)] if residual else []
    res_out_shape = [jax.ShapeDtypeStruct((B, L, D_MODEL), F32)] if residual else []
    return pl.pallas_call(
        functools.partial(_pre_kernel, residual=residual),
        grid=grid,
        in_specs=res_specs + [
                  pl.BlockSpec((1, tm, D_MODEL), tok),
                  modspec,
                  _const_spec((1, D_MODEL)),
                  _const_spec((D_MODEL, C_END)),
                  _const_spec((1, Q_LORA)),
                  _const_spec((Q_LORA, ATTN_HEADS * QK_PAD)),
                  _const_spec((Q_LORA, ATTN_HEADS * 128)),
                  _const_spec((1, KV_LORA)),
                  _const_spec((KV_LORA, ATTN_HEADS * NOPE_DIM)),
                  _const_spec((ATTN_HEADS * V_DIM, KV_LORA)),
                  pl.BlockSpec((tm, 128), lambda b, i: (i, 0)),
                  pl.BlockSpec((tm, 128), lambda b, i: (i, 0))],
        out_specs=[pl.BlockSpec((1, ATTN_HEADS, tm, QK_PAD), head),
                   pl.BlockSpec((1, ATTN_HEADS, tm, QK_PAD), head),
                   pl.BlockSpec((1, ATTN_HEADS, V_DIM, tm), lambda b, i: (b, 0, 0, i)),
                   pl.BlockSpec((1, tm, SSD_INNER), tok),
                   pl.BlockSpec((1, tm, CONV_DIM), tok),
                   pl.BlockSpec((1, tm, POOL_DIM), tok),
                   pl.BlockSpec((1, tm, 128), tok)] + res_out_specs,
        out_shape=[jax.ShapeDtypeStruct((B, ATTN_HEADS, L, QK_PAD), BF16),
                   jax.ShapeDtypeStruct((B, ATTN_HEADS, L, QK_PAD), BF16),
                   jax.ShapeDtypeStruct((B, ATTN_HEADS, V_DIM, L), BF16),
                   jax.ShapeDtypeStruct((B, L, SSD_INNER), F32),
                   jax.ShapeDtypeStruct((B, L, CONV_DIM), F32),
                   jax.ShapeDtypeStruct((B, L, POOL_DIM), F32),
                   jax.ShapeDtypeStruct((B, L, 128), F32)] + res_out_shape,
        compiler_params=_cparams("parallel", "parallel"),
        name="pre_mixer",
    )(*res_args, x, mod, w["n1g"], w["wcat"], w["qg"], w["wq"], w["wqs"], w["kvg"], w["wk"], w["wv"],
      cos_t, sin_t)


ATTN_SLOTS = 4


def _attn_kernel(q_ref, k_ref, vt_ref, o_ref, st_sc, p_sc, *, tk, nk):
    q = q_ref[0, 0]
    tq = q.shape[0]
    ntrip = nk // ATTN_SLOTS

    def chunk_start(j):
        return j * tk if isinstance(j, int) else pl.multiple_of(j * tk, tk)

    def scores(j):
        st = _dot_nt(k_ref[0, 0, pl.ds(chunk_start(j), tk), :], q)
        return st, jnp.max(st, axis=0, keepdims=True)

    def weighted_values(j, slot):
        return _dot(vt_ref[0, 0, :, pl.ds(chunk_start(j), tk)], p_sc[slot])

    def trip(jj, carry, first, last):
        for u in range(ATTN_SLOTS):
            j = ATTN_SLOTS * jj + u
            m_prev, l_prev, acc, a_prev, cmax, cmax_1 = carry
            if not (first and u == 0):
                acc = a_prev * acc + weighted_values(j - 1, (u - 1) % 2)
            if last and u >= ATTN_SLOTS - 2:
                cmax_2 = cmax_1
            else:
                st, cmax_2 = scores(j + 2)
                st_sc[(u + 2) % ATTN_SLOTS] = st
            m_new = jnp.maximum(m_prev, cmax)
            alpha = jnp.exp2(m_prev - m_new)
            p = jnp.exp2(st_sc[u] - m_new)
            l_new = alpha * l_prev + jnp.sum(p, axis=0, keepdims=True)
            p_sc[u % 2] = p.astype(BF16)
            carry = (m_new, l_new, acc, alpha, cmax_1, cmax_2)
        return carry

    st_sc[0], cmax0 = scores(0)
    st_sc[1], cmax1 = scores(1)
    carry = (jnp.full((1, tq), -jnp.inf, F32), jnp.zeros((1, tq), F32), jnp.zeros((V_DIM, tq), F32),
             jnp.ones((1, tq), F32), cmax0, cmax1)
    carry = trip(0, carry, True, ntrip == 1)
    for jj in range(1, ntrip - 1):
        carry = trip(jj, carry, False, False)
    if ntrip > 1:
        carry = trip(ntrip - 1, carry, False, True)
    _, l, acc, a_last, _, _ = carry
    acc = a_last * acc + weighted_values(nk - 1, (ATTN_SLOTS - 1) % 2)
    o_ref[0] = (acc / l).T.astype(o_ref.dtype)


def _attention(q, k, vt, tq, tk):
    B, H, L, _ = q.shape
    nk = L // tk
    assert nk % ATTN_SLOTS == 0, (L, tk)
    kern = functools.partial(_attn_kernel, tk=tk, nk=nk)
    return pl.pallas_call(
        kern,
        grid=(B, H, L // tq),
        in_specs=[pl.BlockSpec((1, 1, tq, QK_PAD), lambda b, h, i: (b, h, i, 0)),
                  pl.BlockSpec((1, 1, L, QK_PAD), lambda b, h, i: (b, h, 0, 0)),
                  pl.BlockSpec((1, 1, V_DIM, L), lambda b, h, i: (b, h, 0, 0))],
        out_specs=pl.BlockSpec((1, tq, V_DIM), lambda b, h, i: (b, i, h)),
        out_shape=jax.ShapeDtypeStruct((B, L, H * V_DIM), BF16),
        scratch_shapes=[pltpu.VMEM((ATTN_SLOTS, tk, tq), F32), pltpu.VMEM((2, tk, tq), BF16)],
        compiler_params=_cparams("parallel", "parallel", "arbitrary"),
        name="attention",
    )(q, k, vt)


def _halo_ext(prev_ref, cur_ref, next_ref, i, nblk):
    prev = jnp.where(i > 0, prev_ref[0], 0.0)
    nxt = jnp.where(i < nblk - 1, next_ref[0], 0.0)
    return jnp.concatenate([prev, cur_ref[0], nxt], axis=0)


def _shift_rows(ext, d, lb):
    n = ext.shape[0]
    r = ext if d == 0 else pltpu.roll(ext, (-d) % n, 0)
    return r[HALO:HALO + lb]


def _ssd_kernel(*refs, reverse, final, nblk, nchunk):
    if final:
        (xp_ref, xc_ref, xn_ref, kd_ref, cw_ref, cb_ref, dtb_ref, a_ref, yf_ref, z_ref, dsk_ref, ng_ref,
         o_ref, st_sc) = refs
    else:
        xp_ref, xc_ref, xn_ref, kd_ref, cw_ref, cb_ref, dtb_ref, a_ref, o_ref, st_sc = refs
    step = pl.program_id(1)
    blk = (nblk - 1 - step) if reverse else step
    lb = nchunk * SSD_CHUNK

    @pl.when(step == 0)
    def _():
        st_sc[...] = jnp.zeros(st_sc.shape, F32)

    ext = _halo_ext(xp_ref, xc_ref, xn_ref, blk, nblk)
    conv = cb_ref[...] + sum(cw_ref[kk:kk + 1, :] * _shift_rows(ext, kk - D_CONV // 2, lb) for kk in range(D_CONV))
    act = _silu(conv)
    dt_all = jax.nn.softplus(kd_ref[0] + dtb_ref[...])
    da_all = dt_all * a_ref[...]

    row = lax.broadcasted_iota(I32, (SSD_CHUNK, SSD_CHUNK), 0)
    col = lax.broadcasted_iota(I32, (SSD_CHUNK, SSD_CHUNK), 1)
    keep = (row <= col) if reverse else (row >= col)
    tri = jnp.where(keep, 1.0, 0.0).astype(F32)
    lane = lax.broadcasted_iota(I32, (SSD_CHUNK, 128), 1)
    low = lane < SSD_HEAD_DIM
    dlane = DT_LANE + (SSD_HEADS if reverse else 0)
    edge = 0 if reverse else SSD_CHUNK - 1

    def pair(arr, h0):
        return jnp.where(low, arr[:, dlane + h0:dlane + h0 + 1], arr[:, dlane + h0 + 1:dlane + h0 + 2])

    ys = []
    for cc in (range(nchunk - 1, -1, -1) if reverse else range(nchunk)):
        r0 = cc * SSD_CHUNK
        da = da_all[r0:r0 + SSD_CHUNK]
        dt = dt_all[r0:r0 + SSD_CHUNK]
        acum = jnp.dot(tri, da, preferred_element_type=F32, precision=lax.Precision.HIGHEST)
        acum_t = acum.T
        a_edge = acum[edge:edge + 1, :]
        ycols = [None] * (SSD_HEADS // 2)
        for g in range(SSD_GROUPS):
            bg = act[r0:r0 + SSD_CHUNK, SSD_INNER + g * SSD_STATE:SSD_INNER + (g + 1) * SSD_STATE]
            cg = act[r0:r0 + SSD_CHUNK, SSD_INNER + (SSD_GROUPS + g) * SSD_STATE:
                     SSD_INNER + (SSD_GROUPS + g + 1) * SSD_STATE]
            cgb = cg.astype(BF16)
            cb = _dot_nt(cgb, bg.astype(BF16))
            bgt = bg.T.astype(BF16)
            for pp in range(SSD_HEADS // SSD_GROUPS // 2):
                pi = g * 2 + pp
                h0 = 2 * pi
                xdt = act[r0:r0 + SSD_CHUNK, pi * 128:(pi + 1) * 128] * pair(dt, h0)
                xdtb = xdt.astype(BF16)
                yd = []
                for hh in (h0, h0 + 1):
                    seg = acum[:, dlane + hh:dlane + hh + 1] - acum_t[dlane + hh:dlane + hh + 1, :]
                    decay = jnp.exp(jnp.where(keep, seg, -jnp.inf))
                    yd.append(_dot((cb * decay).astype(BF16), xdtb))
                y_diag = jnp.where(low, yd[0], yd[1])
                st = st_sc[pi]
                y_off = _dot(cgb, st.astype(BF16)) * jnp.exp(pair(acum, h0))
                ycols[pi] = y_diag + y_off
                to_edge = jnp.exp(pair(a_edge - acum, h0))
                st_sc[pi] = st * jnp.exp(pair(a_edge, h0)) + _dot(bgt, (xdt * to_edge).astype(BF16))
        ys.append((cc, jnp.concatenate(ycols, axis=1)))
    y = jnp.concatenate([v for _, v in sorted(ys, key=lambda t: t[0])], axis=0)

    if final:
        tot = yf_ref[0] + y + act[:, 0:SSD_INNER] * dsk_ref[...]
        o_ref[0] = _rms(tot * _silu(z_ref[0]), ng_ref[...]).astype(o_ref.dtype)
    else:
        o_ref[0] = y


def _ssd_pass(xbc, kd, w, nchunk, reverse, yf=None, z=None):
    B, L, _ = xbc.shape
    lb = nchunk * SSD_CHUNK
    nblk = L // lb
    hb = lb // HALO
    final = yf is not None
    bi = (lambda i: nblk - 1 - i) if reverse else (lambda i: i)
    cur = lambda b, i: (b, bi(i), 0)
    prv = lambda b, i: (b, jnp.maximum(bi(i) * hb - 1, 0), 0)
    nxt = lambda b, i: (b, jnp.minimum((bi(i) + 1) * hb, L // HALO - 1), 0)
    in_specs = [pl.BlockSpec((1, HALO, CONV_DIM), prv),
                pl.BlockSpec((1, lb, CONV_DIM), cur),
                pl.BlockSpec((1, HALO, CONV_DIM), nxt),
                pl.BlockSpec((1, lb, 128), cur),
                _const_spec((D_CONV, CONV_DIM)), _const_spec((1, CONV_DIM)),
                _const_spec((1, 128)), _const_spec((1, 128))]
    args = [xbc, xbc, xbc, kd, w["conv_w"], w["conv_b"], w["dt_bias"], w["a_neg"]]
    if final:
        in_specs += [pl.BlockSpec((1, lb, SSD_INNER), cur), pl.BlockSpec((1, lb, SSD_INNER), cur),
                     _const_spec((1, SSD_INNER)), _const_spec((1, SSD_INNER))]
        args += [yf, z, w["d_skip"], w["ssd_g"]]
    kern = functools.partial(_ssd_kernel, reverse=reverse, final=final, nblk=nblk, nchunk=nchunk)
    return pl.pallas_call(
        kern,
        grid=(B, nblk),
        in_specs=in_specs,
        out_specs=pl.BlockSpec((1, lb, SSD_INNER), cur),
        out_shape=jax.ShapeDtypeStruct((B, L, SSD_INNER), BF16 if final else F32),
        scratch_shapes=[pltpu.VMEM((SSD_HEADS // 2, SSD_STATE, 128), F32)],
        compiler_params=_cparams("parallel", "arbitrary"),
        name="ssd_bwd" if reverse else "ssd_fwd",
    )(*args)


def _pool_kernel(up_ref, uc_ref, un_ref, pw_ref, ps_ref, o_ref, *, nblk, lb, seq):
    blk = pl.program_id(1)
    ext = _halo_ext(up_ref, uc_ref, un_ref, blk, nblk)
    n = ext.shape[0]
    t = blk * lb + lax.broadcasted_iota(I32, (lb, 1), 0)
    for gi, wdw in enumerate(POOL_WINDOWS):
        e = ext[:, gi * POOL_GROUP:(gi + 1) * POOL_GROUP]
        run, width = e, 1
        while width < wdw:
            run = run + pltpu.roll(run, width, 0)
            width *= 2
        tot = _shift_rows(run, wdw // 2 - 1, lb)
        cnt = jnp.minimum(t + wdw // 2, seq) - jnp.maximum(t - wdw // 2, 0)
        d = tot / cnt.astype(F32) - e[HALO:HALO + lb]
        y = _dot(d.astype(BF16), pw_ref[gi])
        o_ref[0, :, gi * POOL_GROUP:(gi + 1) * POOL_GROUP] = (
            y * ps_ref[:, gi * POOL_GROUP:(gi + 1) * POOL_GROUP]).astype(o_ref.dtype)


def _pool(u, w, lb):
    B, L, _ = u.shape
    nblk = L // lb
    hb = lb // HALO
    cur = lambda b, i: (b, i, 0)
    prv = lambda b, i: (b, jnp.maximum(i * hb - 1, 0), 0)
    nxt = lambda b, i: (b, jnp.minimum((i + 1) * hb, L // HALO - 1), 0)
    kern = functools.partial(_pool_kernel, nblk=nblk, lb=lb, seq=L)
    return pl.pallas_call(
        kern,
        grid=(B, nblk),
        in_specs=[pl.BlockSpec((1, HALO, POOL_DIM), prv),
                  pl.BlockSpec((1, lb, POOL_DIM), cur),
                  pl.BlockSpec((1, HALO, POOL_DIM), nxt),
                  _const_spec((len(POOL_WINDOWS), POOL_GROUP, POOL_GROUP)),
                  _const_spec((1, POOL_DIM))],
        out_specs=pl.BlockSpec((1, lb, POOL_DIM), cur),
        out_shape=jax.ShapeDtypeStruct((B, L, POOL_DIM), BF16),
        compiler_params=_cparams("parallel", "parallel"),
        name="pool",
    )(u, u, u, w["pool_w"], w["pool_scale"])


def _post_kernel(x_ref, mod_ref, at_ref, sd_ref, po_ref, woa_ref, wos_ref, wop_ref, n2g_ref, wpq_ref,
                 x1_ref, h2_ref, qp_ref):
    g1 = mod_ref[0, 2:3, :]
    sh2 = mod_ref[0, 3:4, :]
    sc2 = mod_ref[0, 4:5, :]
    mix = _dot(at_ref[0], woa_ref[...]) + _dot(sd_ref[0], wos_ref[...]) + _dot(po_ref[0], wop_ref[...])
    x1 = x_ref[0] + g1 * mix
    x1_ref[0] = x1
    h2 = (_rms(x1, n2g_ref[...]) * (1.0 + sc2) + sh2).astype(BF16)
    h2_ref[0] = h2
    qp_ref[0] = _dot(h2, wpq_ref[...]).astype(BF16)


def _post_mixer(x, mod, attn, ssd, pool, w, tm):
    B, L, _ = x.shape
    tok = lambda b, i: (b, i, 0)
    return pl.pallas_call(
        _post_kernel,
        grid=(B, L // tm),
        in_specs=[pl.BlockSpec((1, tm, D_MODEL), tok),
                  pl.BlockSpec((1, 6, D_MODEL), lambda b, i: (b, 0, 0)),
                  pl.BlockSpec((1, tm, ATTN_OUT), tok),
                  pl.BlockSpec((1, tm, SSD_INNER), tok),
                  pl.BlockSpec((1, tm, POOL_DIM), tok),
                  _const_spec((ATTN_OUT, D_MODEL)), _const_spec((SSD_INNER, D_MODEL)),
                  _const_spec((POOL_DIM, D_MODEL)), _const_spec((1, D_MODEL)),
                  _const_spec((D_MODEL, D_MODEL))],
        out_specs=[pl.BlockSpec((1, tm, D_MODEL), tok)] * 3,
        out_shape=[jax.ShapeDtypeStruct((B, L, D_MODEL), F32),
                   jax.ShapeDtypeStruct((B, L, D_MODEL), BF16),
                   jax.ShapeDtypeStruct((B, L, D_MODEL), BF16)],
        compiler_params=_cparams("parallel", "parallel"),
        name="post_mixer",
    )(x, mod, attn, ssd, pool, w["wo_a"], w["wo_s"], w["wo_p"], w["n2g"], w["wpq"])


def _top16(s, out_v, out_i):
    n = s.shape[0]
    rows = lax.broadcasted_iota(I32, s.shape, 0).astype(F32)
    for r in range(PEER_TOPK):
        m = jnp.max(s, axis=0, keepdims=True)
        idx = jnp.min(jnp.where(s == m, rows, float(n)), axis=0, keepdims=True)
        out_v[r:r + 1, :] = m
        out_i[r:r + 1, :] = idx
        s = jnp.where(rows == idx, -jnp.inf, s)


def _topk_kernel(qp_ref, keys_ref, e_ref, g_ref, v1_sc, i1_sc, v2_sc, i2_sc, cv_sc, ci_sc, tv_sc, ti_sc,
                 eo_sc, go_sc):
    cv_sc[...] = jnp.full(cv_sc.shape, -jnp.inf, F32)
    ci_sc[...] = jnp.zeros(ci_sc.shape, F32)
    for hd in range(PEER_HEADS):
        for half, (vs, is_) in enumerate(((v1_sc, i1_sc), (v2_sc, i2_sc))):
            c0 = (hd * 2 + half) * PEER_HALF
            s = _dot_nt(keys_ref[hd * 2 + half], qp_ref[:, c0:c0 + PEER_HALF])
            _top16(s, vs, is_)
        off = 0
        for a, nb in _STAIR:
            cv_sc[off:off + nb, :] = v1_sc[a:a + 1, :] + v2_sc[0:nb, :]
            ci_sc[off:off + nb, :] = i1_sc[a:a + 1, :] * float(N_KEYS) + i2_sc[0:nb, :]
            off += nb
        cand = cv_sc[...]
        cidx = ci_sc[...]
        rows = lax.broadcasted_iota(I32, cand.shape, 0).astype(F32)
        for r in range(PEER_TOPK):
            m = jnp.max(cand, axis=0, keepdims=True)
            pos = jnp.min(jnp.where(cand == m, rows, float(_STAIR_PAD)), axis=0, keepdims=True)
            hit = rows == pos
            tv_sc[r:r + 1, :] = m
            ti_sc[r:r + 1, :] = jnp.sum(jnp.where(hit, cidx, 0.0), axis=0, keepdims=True)
            cand = jnp.where(hit, -jnp.inf, cand)
        tv = tv_sc[...]
        p = jnp.exp(tv - tv[0:1, :])
        go_sc[hd * PEER_TOPK:(hd + 1) * PEER_TOPK, :] = p / jnp.sum(p, axis=0, keepdims=True)
        eo_sc[hd * PEER_TOPK:(hd + 1) * PEER_TOPK, :] = ti_sc[...]
    e_ref[...] = eo_sc[...].T.astype(I32)
    g_ref[...] = go_sc[...].T


def _peer_topk(qp, keys, tm):
    T = qp.shape[0]
    return pl.pallas_call(
        _topk_kernel,
        grid=(T // tm,),
        in_specs=[pl.BlockSpec((tm, D_MODEL), lambda i: (i, 0)),
                  _const_spec((PEER_HEADS * 2, N_KEYS, PEER_HALF))],
        out_specs=[pl.BlockSpec((tm, HK), lambda i: (i, 0))] * 2,
        out_shape=[jax.ShapeDtypeStruct((T, HK), I32), jax.ShapeDtypeStruct((T, HK), F32)],
        scratch_shapes=[pltpu.VMEM((PEER_TOPK, tm), F32)] * 4
                       + [pltpu.VMEM((_STAIR_PAD, tm), F32)] * 2
                       + [pltpu.VMEM((PEER_TOPK, tm), F32)] * 2
                       + [pltpu.VMEM((HK, tm), F32)] * 2,
        compiler_params=_cparams("parallel"),
        name="peer_topk",
    )(qp, keys)


G_PITCH = N_KEYS + 8


G_GROUP = 16


def _gate_kernel(e_ref, g_ref, o_ref, gs_sc, *, tg):
    sub = lax.broadcasted_iota(I32, (N_KEYS, HK), 0)
    ngroup = tg // G_GROUP

    def build(n, slot):
        for u in range(G_GROUP):
            t = n * G_GROUP + u
            er = e_ref[pl.ds(t, 1), :]
            gr = g_ref[pl.ds(t, 1), :]
            i1 = lax.shift_right_logical(er, 7)
            i2 = lax.bitwise_and(er, N_KEYS - 1)
            a = jnp.where(sub == i1, 1.0, 0.0).astype(BF16)
            b = jnp.where(sub == i2, gr, 0.0).astype(BF16)
            gs_sc[slot, pl.ds(u * G_PITCH, N_KEYS), :] = _dot_nt(a, b)

    def regroup(n, slot):
        row = n * G_GROUP if isinstance(n, int) else pl.multiple_of(n * G_GROUP, G_GROUP)
        for c in range(N_KEYS):
            lo = gs_sc[slot, pl.ds(c, 8, stride=G_PITCH), :]
            hi = gs_sc[slot, pl.ds(8 * G_PITCH + c, 8, stride=G_PITCH), :]
            o_ref[pl.ds(row, G_GROUP), c * N_KEYS:(c + 1) * N_KEYS] = (
                jnp.concatenate([lo, hi], axis=0).astype(o_ref.dtype))

    def body(nn, carry):
        n = 2 * nn + 1
        build(n, 1)
        regroup(n - 1, 0)
        build(n + 1, 0)
        regroup(n, 1)
        return carry

    build(0, 0)
    lax.fori_loop(0, (ngroup - 1) // 2, body, 0)
    build(ngroup - 1, 1)
    regroup(ngroup - 2, 0)
    regroup(ngroup - 1, 1)


def _peer_gates(e, g, tg):
    T = e.shape[0]
    assert tg % (2 * G_GROUP) == 0, tg
    kern = functools.partial(_gate_kernel, tg=tg)
    return pl.pallas_call(
        kern,
        grid=(T // tg,),
        in_specs=[pl.BlockSpec((tg, HK), lambda i: (i, 0))] * 2,
        out_specs=pl.BlockSpec((tg, N_EXPERTS), lambda i: (i, 0)),
        out_shape=jax.ShapeDtypeStruct((T, N_EXPERTS), BF16),
        scratch_shapes=[pltpu.VMEM((2, G_GROUP * G_PITCH, N_KEYS), F32)],
        compiler_params=_cparams("parallel"),
        name="peer_gates",
    )(e, g)


def _dense_kernel(h_ref, u_ref, v_ref, gt_ref, o_ref):
    @pl.when(pl.program_id(2) == 0)
    def _():
        o_ref[...] = jnp.zeros(o_ref.shape, F32)

    a = _gelu(_dot_nt(h_ref[0], u_ref[...]))
    wgt = (a * gt_ref[0].astype(F32)).astype(BF16)
    o_ref[0] += _dot(wgt, v_ref[...])


def _peer_dense(h2, u, v, gates, tb, eb):
    B, L, _ = h2.shape
    tok = lambda b, i, j: (b, i, 0)
    return pl.pallas_call(
        _dense_kernel,
        grid=(B, L // tb, N_EXPERTS // eb),
        in_specs=[pl.BlockSpec((1, tb, D_MODEL), tok),
                  pl.BlockSpec((eb, D_MODEL), lambda b, i, j: (j, 0)),
                  pl.BlockSpec((eb, D_MODEL), lambda b, i, j: (j, 0)),
                  pl.BlockSpec((1, tb, eb), lambda b, i, j: (b, i, j))],
        out_specs=pl.BlockSpec((1, tb, D_MODEL), tok),
        out_shape=jax.ShapeDtypeStruct((B, L, D_MODEL), F32),
        compiler_params=_cparams("parallel", "parallel", "arbitrary"),
        name="peer_dense",
    )(h2, u, v, gates)


def _final_kernel(x1_ref, pe_ref, mod_ref, fg_ref, o_ref):
    o_ref[0] = _rms(x1_ref[0] + mod_ref[0, 5:6, :] * pe_ref[0], fg_ref[...])


def _final_norm(x1, peer, mod, fg, tm):
    B, L, _ = x1.shape
    tok = lambda b, i: (b, i, 0)
    return pl.pallas_call(
        _final_kernel,
        grid=(B, L // tm),
        in_specs=[pl.BlockSpec((1, tm, D_MODEL), tok), pl.BlockSpec((1, tm, D_MODEL), tok),
                  pl.BlockSpec((1, 6, D_MODEL), lambda b, i: (b, 0, 0)), _const_spec((1, D_MODEL))],
        out_specs=pl.BlockSpec((1, tm, D_MODEL), tok),
        out_shape=jax.ShapeDtypeStruct((B, L, D_MODEL), F32),
        compiler_params=_cparams("parallel", "parallel"),
        name="final_norm",
    )(x1, peer, mod, fg)


def _rope_tables(L):
    inv = 1.0 / (ROPE_BASE ** (jnp.arange(0, ROPE_DIM, 2, dtype=F32) / ROPE_DIM))
    ang = jnp.arange(L, dtype=F32)[:, None] * inv[None, :]
    cos, sin = jnp.cos(ang), jnp.sin(ang)
    zero = jnp.zeros((L, 128 - ROPE_DIM), F32)
    return jnp.concatenate([cos, cos, zero], axis=1), jnp.concatenate([-sin, sin, zero], axis=1)


def _swap_halves(w):
    half = w.shape[-1] // 2
    return jnp.concatenate([w[..., half:], w[..., :half]], axis=-1)


def _layer_weights(p, i):
    w_in = p["w_in"][i]
    s0, s1, s2, s3, s4, s5 = (Q_LORA, Q_LORA + KV_LORA, Q_LORA + KV_LORA + ROPE_DIM,
                              Q_LORA + KV_LORA + ROPE_DIM + SSD_INNER,
                              Q_LORA + KV_LORA + ROPE_DIM + SSD_INNER + CONV_DIM,
                              Q_LORA + KV_LORA + ROPE_DIM + SSD_INNER + CONV_DIM + 2 * SSD_HEADS)
    w_kr = w_in[:, s1:s2]
    zcol = lambda n: jnp.zeros((D_MODEL, n), F32)
    wcat = jnp.concatenate(
        [w_in[:, :s0], w_in[:, s0:s1], w_in[:, s2:s3], w_in[:, s3:s4], w_in[:, s5:],
         w_kr, w_in[:, s4:s5], zcol(128 - ROPE_DIM - 2 * SSD_HEADS),
         _swap_halves(w_kr), zcol(128 - ROPE_DIM)], axis=1).astype(BF16)
    wqb = p["w_q_b"][i].reshape(Q_LORA, ATTN_HEADS, NOPE_DIM + ROPE_DIM)
    zq = jnp.zeros((Q_LORA, ATTN_HEADS, 128 - ROPE_DIM), F32)
    wq = jnp.concatenate([wqb, zq], axis=2).reshape(Q_LORA, ATTN_HEADS * QK_PAD).astype(BF16)
    wqs = jnp.concatenate([_swap_halves(wqb[:, :, NOPE_DIM:]), zq], axis=2).reshape(
        Q_LORA, ATTN_HEADS * 128).astype(BF16)
    wkv = p["w_kv_b"][i].reshape(KV_LORA, ATTN_HEADS, NOPE_DIM + V_DIM)
    wk = wkv[:, :, :NOPE_DIM].reshape(KV_LORA, ATTN_HEADS * NOPE_DIM).astype(BF16)
    wv = wkv[:, :, NOPE_DIM:].reshape(KV_LORA, ATTN_HEADS * V_DIM).T.astype(BF16)
    lane_pad = lambda f, b: jnp.concatenate(
        [jnp.zeros((DT_LANE,), F32), f, b, jnp.zeros((128 - DT_LANE - 2 * SSD_HEADS,), F32)])[None, :]
    w_out = p["w_out"][i].astype(BF16)
    return dict(
        n1g=p["norm1_g"][i][None, :], n2g=p["norm2_g"][i][None, :],
        wcat=wcat, qg=p["q_a_norm_g"][i][None, :], wq=wq, wqs=wqs,
        kvg=p["kv_a_norm_g"][i][None, :], wk=wk, wv=wv,
        conv_w=p["conv_w"][i], conv_b=p["conv_b"][i][None, :],
        dt_bias=lane_pad(p["dt_bias_fwd"][i], p["dt_bias_bwd"][i]),
        a_neg=lane_pad(-jnp.exp(p["a_log_fwd"][i]), -jnp.exp(p["a_log_bwd"][i])),
        d_skip=jnp.repeat(p["d_skip"][i], SSD_HEAD_DIM)[None, :], ssd_g=p["ssd_norm_g"][i][None, :],
        pool_w=p["pool_w"][i].astype(BF16), pool_scale=p["pool_scale"][i][None, :],
        wo_a=w_out[:ATTN_OUT], wo_s=w_out[ATTN_OUT:ATTN_OUT + SSD_INNER], wo_p=w_out[ATTN_OUT + SSD_INNER:],
        wpq=p["peer_wq"][i].astype(BF16),
        keys=p["peer_keys"][i].reshape(PEER_HEADS * 2, N_KEYS, PEER_HALF).astype(BF16),
        u=p["peer_u"][i].astype(BF16), v=p["peer_v"][i].astype(BF16),
    )


def _block(n, pref):
    for c in pref:
        if n % c == 0:
            return c
    raise ValueError(f"no block size in {pref} divides {n}")


def _encoder(x, mods, weights, fg):
    B, L, _ = x.shape
    cos_t, sin_t = _rope_tables(L)
    tm = _block(L, (256, 128))
    tq = _block(L, (512, 256, 128))
    tk = _block(L // ATTN_SLOTS, (1024, 512, 256, 128))
    nchunk = _block(L // SSD_CHUNK, (4, 2, 1))
    lp = _block(L, (512, 256, 128))
    tb = _block(L, (1024, 512, 256, 128))
    tt = _block(B * L, (512, 256, 128))
    tg = _block(B * L, (64, 32))
    peer = None
    for i in range(DEPTH):
        w, mod = weights[i], mods[i]
        if peer is None:
            q, k, v, z, xbc, pool_in, kd = _pre_mixer(x, mod, w, cos_t, sin_t, tm)
        else:
            q, k, v, z, xbc, pool_in, kd, x = _pre_mixer(x1, mod, w, cos_t, sin_t, tm, peer, mods[i - 1])
        attn = _attention(q, k, v, tq, tk)
        yf = _ssd_pass(xbc, kd, w, nchunk, reverse=False)
        ssd = _ssd_pass(xbc, kd, w, nchunk, reverse=True, yf=yf, z=z)
        pool = _pool(pool_in, w, lp)
        x1, h2, qp = _post_mixer(x, mod, attn, ssd, pool, w, tm)
        e, g = _peer_topk(qp.reshape(B * L, D_MODEL), w["keys"], tt)
        gates = _peer_gates(e, g, tg).reshape(B, L, N_EXPERTS)
        peer = _peer_dense(h2, w["u"], w["v"], gates, tb, 1024)
    return _final_norm(x1, peer, mods[DEPTH - 1], fg, tm)


def kernel(x_prompt, x_sample, c_prompt, c_sample, mod_w, mod_b, norm1_g, norm2_g, w_in, q_a_norm_g, w_q_b, kv_a_norm_g, w_kv_b, conv_w, conv_b, a_log_fwd, a_log_bwd, dt_bias_fwd, dt_bias_bwd, d_skip, ssd_norm_g, pool_w, pool_scale, w_out, peer_wq, peer_keys, peer_u, peer_v, final_norm_g):
    p = dict(mod_w=mod_w, mod_b=mod_b, norm1_g=norm1_g, norm2_g=norm2_g, w_in=w_in,
             q_a_norm_g=q_a_norm_g, w_q_b=w_q_b, kv_a_norm_g=kv_a_norm_g, w_kv_b=w_kv_b,
             conv_w=conv_w, conv_b=conv_b, a_log_fwd=a_log_fwd, a_log_bwd=a_log_bwd,
             dt_bias_fwd=dt_bias_fwd, dt_bias_bwd=dt_bias_bwd, d_skip=d_skip, ssd_norm_g=ssd_norm_g,
             pool_w=pool_w, pool_scale=pool_scale, w_out=w_out, peer_wq=peer_wq, peer_keys=peer_keys,
             peer_u=peer_u, peer_v=peer_v)
    weights = [_layer_weights(p, i) for i in range(DEPTH)]
    bp, bs = c_prompt.shape[0], c_sample.shape[0]
    c_pad = jnp.concatenate([c_prompt, c_sample, jnp.zeros((8 - bp - bs, D_MODEL), F32)], axis=0)
    mods = [_modulation(c_pad, mod_w[i], mod_b[i][None, :]) for i in range(DEPTH)]
    fg = final_norm_g[None, :]
    mods_p = [m[:bp].reshape(bp, 6, D_MODEL) for m in mods]
    mods_s = [m[bp:bp + bs].reshape(bs, 6, D_MODEL) for m in mods]
    return (_encoder(x_prompt, mods_p, weights, fg), _encoder(x_sample, mods_s, weights, fg))
```

```python
import functools
import math

import jax
import jax.numpy as jnp
from jax import lax
from jax.experimental import pallas as pl
from jax.experimental.pallas import tpu as pltpu

F32 = jnp.float32
BF16 = jnp.bfloat16
I32 = jnp.int32

D_MODEL = 2048
DEPTH = 2
EPS = 1e-6
ATTN_HEADS = 8
Q_LORA = 512
KV_LORA = 256
NOPE_DIM = 128
ROPE_DIM = 64
V_DIM = 128
ROPE_BASE = 10000.0
QK_PAD = 256
SSD_HEADS = 8
SSD_HEAD_DIM = 64
SSD_INNER = SSD_HEADS * SSD_HEAD_DIM
SSD_GROUPS = 2
SSD_STATE = 128
SSD_CHUNK = 128
D_CONV = 4
CONV_DIM = SSD_INNER + 2 * SSD_GROUPS * SSD_STATE
POOL_WINDOWS = (2, 4, 8, 16)
POOL_GROUP = 128
POOL_DIM = len(POOL_WINDOWS) * POOL_GROUP
ATTN_OUT = ATTN_HEADS * V_DIM
PEER_HEADS = 8
N_KEYS = 128
N_EXPERTS = N_KEYS * N_KEYS
PEER_HALF = 128
PEER_TOPK = 16
HK = PEER_HEADS * PEER_TOPK
HALO = 8
DT_LANE = 64

C_CQ = 0
C_CKV = C_CQ + Q_LORA
C_Z = C_CKV + KV_LORA
C_XBC = C_Z + SSD_INNER
C_POOL = C_XBC + CONV_DIM
C_KD = C_POOL + POOL_DIM
C_KDS = C_KD + 128
C_END = C_KDS + 128

VMEM_LIMIT = 56 * 1024 * 1024

_STAIR = [(a, PEER_TOPK // (a + 1)) for a in range(PEER_TOPK)]
_STAIR_ROWS = sum(n for _, n in _STAIR)
_STAIR_PAD = ((_STAIR_ROWS + 7) // 8) * 8


def _cparams(*sem):
    return pltpu.CompilerParams(dimension_semantics=sem, vmem_limit_bytes=VMEM_LIMIT)


def _const_spec(shape):
    nd = len(shape)
    return pl.BlockSpec(shape, lambda *_: (0,) * nd, pipeline_mode=pl.Buffered(1))


def _rms(x, g):
    return x * lax.rsqrt(jnp.mean(x * x, axis=-1, keepdims=True) + EPS) * g


def _silu(x):
    return x * jax.nn.sigmoid(x)


def _gelu(x):
    return 0.5 * x * (1.0 + lax.erf(x * (1.0 / math.sqrt(2.0))))


def _dot(a, b):
    return jnp.dot(a, b, preferred_element_type=F32)


def _dot_nt(a, b):
    return lax.dot_general(a, b, (((1,), (1,)), ((), ())), preferred_element_type=F32)


def _mod_kernel(c_ref, w_ref, b_ref, o_ref):
    cs = _silu(c_ref[...])
    o_ref[...] = _dot(cs.astype(BF16), w_ref[...].astype(BF16)) + b_ref[...]


def _modulation(c_pad, mod_w, mod_b):
    bn = 1024
    n = mod_w.shape[1]
    return pl.pallas_call(
        _mod_kernel,
        grid=(n // bn,),
        in_specs=[pl.BlockSpec((8, D_MODEL), lambda j: (0, 0)),
                  pl.BlockSpec((D_MODEL, bn), lambda j: (0, j)),
                  pl.BlockSpec((1, bn), lambda j: (0, j))],
        out_specs=pl.BlockSpec((8, bn), lambda j: (0, j)),
        out_shape=jax.ShapeDtypeStruct((8, n), F32),
        compiler_params=_cparams("parallel"),
        name="modulation",
    )(c_pad, mod_w, mod_b)


def _pre_kernel(*refs, residual):
    if residual:
        pe_ref, pmod_ref, x_ref, *refs = refs
    else:
        x_ref, *refs = refs
    (mod_ref, n1g_ref, wcat_ref, qg_ref, wq_ref, wqs_ref, kvg_ref, wk_ref, wv_ref, cos_ref, sin_ref,
     q_ref, k_ref, v_ref, z_ref, xbc_ref, pool_ref, kd_ref, *xo_ref) = refs
    x = x_ref[0]
    if residual:
        x = x + pmod_ref[0, 5:6, :] * pe_ref[0]
        xo_ref[0][0] = x
    sh1 = mod_ref[0, 0:1, :]
    sc1 = mod_ref[0, 1:2, :]
    h = _rms(x, n1g_ref[...]) * (1.0 + sc1) + sh1
    proj = _dot(h.astype(BF16), wcat_ref[...])
    z_ref[0] = proj[:, C_Z:C_XBC]
    xbc_ref[0] = proj[:, C_XBC:C_POOL]
    pool_ref[0] = proj[:, C_POOL:C_KD]
    kd = proj[:, C_KD:C_KDS]
    kd_ref[0] = kd
    cos = cos_ref[...]
    sin = sin_ref[...]
    scale = math.log2(math.e) / math.sqrt(NOPE_DIM + ROPE_DIM)

    cqn = _rms(proj[:, C_CQ:C_CKV], qg_ref[...]).astype(BF16)
    qm = _dot(cqn, wq_ref[...])
    qs = _dot(cqn, wqs_ref[...])
    for hd in range(ATTN_HEADS):
        o = hd * QK_PAD
        q_ref[0, hd, :, 0:NOPE_DIM] = (qm[:, o:o + NOPE_DIM] * scale).astype(BF16)
        rope = qm[:, o + NOPE_DIM:o + QK_PAD] * cos + qs[:, hd * 128:(hd + 1) * 128] * sin
        q_ref[0, hd, :, NOPE_DIM:QK_PAD] = (rope * scale).astype(BF16)

    ckvn = _rms(proj[:, C_CKV:C_Z], kvg_ref[...]).astype(BF16)
    kn = _dot(ckvn, wk_ref[...])
    vvt = _dot_nt(wv_ref[...], ckvn)
    krope = (kd * cos + proj[:, C_KDS:C_END] * sin).astype(BF16)
    for hd in range(ATTN_HEADS):
        k_ref[0, hd, :, 0:NOPE_DIM] = kn[:, hd * NOPE_DIM:(hd + 1) * NOPE_DIM].astype(BF16)
        k_ref[0, hd, :, NOPE_DIM:QK_PAD] = krope
        v_ref[0, hd] = vvt[hd * V_DIM:(hd + 1) * V_DIM, :].astype(BF16)


def _pre_mixer(x, mod, w, cos_t, sin_t, tm, peer=None, prev_mod=None):
    B, L, _ = x.shape
    grid = (B, L // tm)
    tok = lambda b, i: (b, i, 0)
    head = lambda b, i: (b, 0, i, 0)
    modspec = pl.BlockSpec((1, 6, D_MODEL), lambda b, i: (b, 0, 0))
    residual = peer is not None
    res_specs = [pl.BlockSpec((1, tm, D_MODEL), tok), modspec] if residual else []
    res_args = [peer, prev_mod] if residual else []
    res_out_specs = [pl.BlockSpec((1, tm, D_MODEL), tok)] if residual else []
    res_out_shape = [jax.ShapeDtypeStruct((B, L, D_MODEL), F32)] if residual else []
    return pl.pallas_call(
        functools.partial(_pre_kernel, residual=residual),
        grid=grid,
        in_specs=res_specs + [
                  pl.BlockSpec((1, tm, D_MODEL), tok),
                  modspec,
                  _const_spec((1, D_MODEL)),
                  _const_spec((D_MODEL, C_END)),
                  _const_spec((1, Q_LORA)),
                  _const_spec((Q_LORA, ATTN_HEADS * QK_PAD)),
                  _const_spec((Q_LORA, ATTN_HEADS * 128)),
                  _const_spec((1, KV_LORA)),
                  _const_spec((KV_LORA, ATTN_HEADS * NOPE_DIM)),
                  _const_spec((ATTN_HEADS * V_DIM, KV_LORA)),
                  pl.BlockSpec((tm, 128), lambda b, i: (i, 0)),
                  pl.BlockSpec((tm, 128), lambda b, i: (i, 0))],
        out_specs=[pl.BlockSpec((1, ATTN_HEADS, tm, QK_PAD), head),
                   pl.BlockSpec((1, ATTN_HEADS, tm, QK_PAD), head),
                   pl.BlockSpec((1, ATTN_HEADS, V_DIM, tm), lambda b, i: (b, 0, 0, i)),
                   pl.BlockSpec((1, tm, SSD_INNER), tok),
                   pl.BlockSpec((1, tm, CONV_DIM), tok),
                   pl.BlockSpec((1, tm, POOL_DIM), tok),
                   pl.BlockSpec((1, tm, 128), tok)] + res_out_specs,
        out_shape=[jax.ShapeDtypeStruct((B, ATTN_HEADS, L, QK_PAD), BF16),
                   jax.ShapeDtypeStruct((B, ATTN_HEADS, L, QK_PAD), BF16),
                   jax.ShapeDtypeStruct((B, ATTN_HEADS, V_DIM, L), BF16),
                   jax.ShapeDtypeStruct((B, L, SSD_INNER), F32),
                   jax.ShapeDtypeStruct((B, L, CONV_DIM), F32),
                   jax.ShapeDtypeStruct((B, L, POOL_DIM), F32),
                   jax.ShapeDtypeStruct((B, L, 128), F32)] + res_out_shape,
        compiler_params=_cparams("parallel", "parallel"),
        name="pre_mixer",
    )(*res_args, x, mod, w["n1g"], w["wcat"], w["qg"], w["wq"], w["wqs"], w["kvg"], w["wk"], w["wv"],
      cos_t, sin_t)


ATTN_SLOTS = 4


def _attn_kernel(q_ref, k_ref, vt_ref, o_ref, st_sc, p_sc, *, tk, nk):
    q = q_ref[0, 0]
    tq = q.shape[0]
    ntrip = nk // ATTN_SLOTS

    def scores(j):
        st = _dot_nt(k_ref[0, 0, j * tk:(j + 1) * tk, :], q)
        return st, jnp.max(st, axis=0, keepdims=True)

    def weighted_values(j, slot):
        return _dot(vt_ref[0, 0, :, j * tk:(j + 1) * tk], p_sc[slot])

    def trip(jj, carry, first, last):
        for u in range(ATTN_SLOTS):
            j = ATTN_SLOTS * jj + u
            m_prev, l_prev, acc, a_prev, cmax, cmax_1 = carry
            if not (first and u == 0):
                acc = a_prev * acc + weighted_values(j - 1, (u - 1) % 2)
            if last and u >= ATTN_SLOTS - 2:
                cmax_2 = cmax_1
            else:
                st, cmax_2 = scores(j + 2)
                st_sc[(u + 2) % ATTN_SLOTS] = st
            m_new = jnp.maximum(m_prev, cmax)
            alpha = jnp.exp2(m_prev - m_new)
            p = jnp.exp2(st_sc[u] - m_new)
            l_new = alpha * l_prev + jnp.sum(p, axis=0, keepdims=True)
            p_sc[u % 2] = p.astype(BF16)
            carry = (m_new, l_new, acc, alpha, cmax_1, cmax_2)
        return carry

    st_sc[0], cmax0 = scores(0)
    st_sc[1], cmax1 = scores(1)
    carry = (jnp.full((1, tq), -jnp.inf, F32), jnp.zeros((1, tq), F32), jnp.zeros((V_DIM, tq), F32),
             jnp.ones((1, tq), F32), cmax0, cmax1)
    carry = trip(0, carry, True, ntrip == 1)
    for jj in range(1, ntrip - 1):
        carry = trip(jj, carry, False, False)
    if ntrip > 1:
        carry = trip(ntrip - 1, carry, False, True)
    _, l, acc, a_last, _, _ = carry
    acc = a_last * acc + weighted_values(nk - 1, (ATTN_SLOTS - 1) % 2)
    o_ref[0] = (acc / l).T.astype(o_ref.dtype)


def _attention(q, k, vt, tq, tk):
    B, H, L, _ = q.shape
    nk = L // tk
    assert nk % ATTN_SLOTS == 0, (L, tk)
    kern = functools.partial(_attn_kernel, tk=tk, nk=nk)
    return pl.pallas_call(
        kern,
        grid=(B, H, L // tq),
        in_specs=[pl.BlockSpec((1, 1, tq, QK_PAD), lambda b, h, i: (b, h, i, 0)),
                  pl.BlockSpec((1, 1, L, QK_PAD), lambda b, h, i: (b, h, 0, 0)),
                  pl.BlockSpec((1, 1, V_DIM, L), lambda b, h, i: (b, h, 0, 0))],
        out_specs=pl.BlockSpec((1, tq, V_DIM), lambda b, h, i: (b, i, h)),
        out_shape=jax.ShapeDtypeStruct((B, L, H * V_DIM), BF16),
        scratch_shapes=[pltpu.VMEM((ATTN_SLOTS, tk, tq), F32), pltpu.VMEM((2, tk, tq), BF16)],
        compiler_params=_cparams("parallel", "parallel", "arbitrary"),
        name="attention",
    )(q, k, vt)


def _halo_ext(prev_ref, cur_ref, next_ref, i, nblk):
    prev = jnp.where(i > 0, prev_ref[0], 0.0)
    nxt = jnp.where(i < nblk - 1, next_ref[0], 0.0)
    return jnp.concatenate([prev, cur_ref[0], nxt], axis=0)


def _shift_rows(ext, d, lb):
    n = ext.shape[0]
    r = ext if d == 0 else pltpu.roll(ext, (-d) % n, 0)
    return r[HALO:HALO + lb]


def _ssd_kernel(*refs, reverse, final, nblk, nchunk):
    if final:
        (xp_ref, xc_ref, xn_ref, kd_ref, cw_ref, cb_ref, dtb_ref, a_ref, yf_ref, z_ref, dsk_ref, ng_ref,
         o_ref, st_sc) = refs
    else:
        xp_ref, xc_ref, xn_ref, kd_ref, cw_ref, cb_ref, dtb_ref, a_ref, o_ref, st_sc = refs
    step = pl.program_id(1)
    blk = (nblk - 1 - step) if reverse else step
    lb = nchunk * SSD_CHUNK

    @pl.when(step == 0)
    def _():
        st_sc[...] = jnp.zeros(st_sc.shape, F32)

    ext = _halo_ext(xp_ref, xc_ref, xn_ref, blk, nblk)
    conv = cb_ref[...] + sum(cw_ref[kk:kk + 1, :] * _shift_rows(ext, kk - D_CONV // 2, lb) for kk in range(D_CONV))
    act = _silu(conv)
    dt_all = jax.nn.softplus(kd_ref[0] + dtb_ref[...])
    da_all = dt_all * a_ref[...]

    row = lax.broadcasted_iota(I32, (SSD_CHUNK, SSD_CHUNK), 0)
    col = lax.broadcasted_iota(I32, (SSD_CHUNK, SSD_CHUNK), 1)
    keep = (row <= col) if reverse else (row >= col)
    tri = jnp.where(keep, 1.0, 0.0).astype(F32)
    lane = lax.broadcasted_iota(I32, (SSD_CHUNK, 128), 1)
    low = lane < SSD_HEAD_DIM
    dlane = DT_LANE + (SSD_HEADS if reverse else 0)
    edge = 0 if reverse else SSD_CHUNK - 1

    def pair(arr, h0):
        return jnp.where(low, arr[:, dlane + h0:dlane + h0 + 1], arr[:, dlane + h0 + 1:dlane + h0 + 2])

    ys = []
    for cc in (range(nchunk - 1, -1, -1) if reverse else range(nchunk)):
        r0 = cc * SSD_CHUNK
        da = da_all[r0:r0 + SSD_CHUNK]
        dt = dt_all[r0:r0 + SSD_CHUNK]
        acum = jnp.dot(tri, da, preferred_element_type=F32, precision=lax.Precision.HIGHEST)
        acum_t = acum.T
        a_edge = acum[edge:edge + 1, :]
        ycols = [None] * (SSD_HEADS // 2)
        for g in range(SSD_GROUPS):
            bg = act[r0:r0 + SSD_CHUNK, SSD_INNER + g * SSD_STATE:SSD_INNER + (g + 1) * SSD_STATE]
            cg = act[r0:r0 + SSD_CHUNK, SSD_INNER + (SSD_GROUPS + g) * SSD_STATE:
                     SSD_INNER + (SSD_GROUPS + g + 1) * SSD_STATE]
            cgb = cg.astype(BF16)
            cb = _dot_nt(cgb, bg.astype(BF16))
            bgt = bg.T.astype(BF16)
            for pp in range(SSD_HEADS // SSD_GROUPS // 2):
                pi = g * 2 + pp
                h0 = 2 * pi
                xdt = act[r0:r0 + SSD_CHUNK, pi * 128:(pi + 1) * 128] * pair(dt, h0)
                xdtb = xdt.astype(BF16)
                yd = []
                for hh in (h0, h0 + 1):
                    seg = acum[:, dlane + hh:dlane + hh + 1] - acum_t[dlane + hh:dlane + hh + 1, :]
                    decay = jnp.exp(jnp.where(keep, seg, -jnp.inf))
                    yd.append(_dot((cb * decay).astype(BF16), xdtb))
                y_diag = jnp.where(low, yd[0], yd[1])
                st = st_sc[pi]
                y_off = _dot(cgb, st.astype(BF16)) * jnp.exp(pair(acum, h0))
                ycols[pi] = y_diag + y_off
                to_edge = jnp.exp(pair(a_edge - acum, h0))
                st_sc[pi] = st * jnp.exp(pair(a_edge, h0)) + _dot(bgt, (xdt * to_edge).astype(BF16))
        ys.append((cc, jnp.concatenate(ycols, axis=1)))
    y = jnp.concatenate([v for _, v in sorted(ys, key=lambda t: t[0])], axis=0)

    if final:
        tot = yf_ref[0] + y + act[:, 0:SSD_INNER] * dsk_ref[...]
        o_ref[0] = _rms(tot * _silu(z_ref[0]), ng_ref[...]).astype(o_ref.dtype)
    else:
        o_ref[0] = y


def _ssd_pass(xbc, kd, w, nchunk, reverse, yf=None, z=None):
    B, L, _ = xbc.shape
    lb = nchunk * SSD_CHUNK
    nblk = L // lb
    hb = lb // HALO
    final = yf is not None
    bi = (lambda i: nblk - 1 - i) if reverse else (lambda i: i)
    cur = lambda b, i: (b, bi(i), 0)
    prv = lambda b, i: (b, jnp.maximum(bi(i) * hb - 1, 0), 0)
    nxt = lambda b, i: (b, jnp.minimum((bi(i) + 1) * hb, L // HALO - 1), 0)
    in_specs = [pl.BlockSpec((1, HALO, CONV_DIM), prv),
                pl.BlockSpec((1, lb, CONV_DIM), cur),
                pl.BlockSpec((1, HALO, CONV_DIM), nxt),
                pl.BlockSpec((1, lb, 128), cur),
                _const_spec((D_CONV, CONV_DIM)), _const_spec((1, CONV_DIM)),
                _const_spec((1, 128)), _const_spec((1, 128))]
    args = [xbc, xbc, xbc, kd, w["conv_w"], w["conv_b"], w["dt_bias"], w["a_neg"]]
    if final:
        in_specs += [pl.BlockSpec((1, lb, SSD_INNER), cur), pl.BlockSpec((1, lb, SSD_INNER), cur),
                     _const_spec((1, SSD_INNER)), _const_spec((1, SSD_INNER))]
        args += [yf, z, w["d_skip"], w["ssd_g"]]
    kern = functools.partial(_ssd_kernel, reverse=reverse, final=final, nblk=nblk, nchunk=nchunk)
    return pl.pallas_call(
        kern,
        grid=(B, nblk),
        in_specs=in_specs,
        out_specs=pl.BlockSpec((1, lb, SSD_INNER), cur),
        out_shape=jax.ShapeDtypeStruct((B, L, SSD_INNER), BF16 if final else F32),
        scratch_shapes=[pltpu.VMEM((SSD_HEADS // 2, SSD_STATE, 128), F32)],
        compiler_params=_cparams("parallel", "arbitrary"),
        name="ssd_bwd" if reverse else "ssd_fwd",
    )(*args)


def _pool_kernel(up_ref, uc_ref, un_ref, pw_ref, ps_ref, o_ref, *, nblk, lb, seq):
    blk = pl.program_id(1)
    ext = _halo_ext(up_ref, uc_ref, un_ref, blk, nblk)
    n = ext.shape[0]
    t = blk * lb + lax.broadcasted_iota(I32, (lb, 1), 0)
    for gi, wdw in enumerate(POOL_WINDOWS):
        e = ext[:, gi * POOL_GROUP:(gi + 1) * POOL_GROUP]
        run, width = e, 1
        while width < wdw:
            run = run + pltpu.roll(run, width, 0)
            width *= 2
        tot = _shift_rows(run, wdw // 2 - 1, lb)
        cnt = jnp.minimum(t + wdw // 2, seq) - jnp.maximum(t - wdw // 2, 0)
        d = tot / cnt.astype(F32) - e[HALO:HALO + lb]
        y = _dot(d.astype(BF16), pw_ref[gi])
        o_ref[0, :, gi * POOL_GROUP:(gi + 1) * POOL_GROUP] = (
            y * ps_ref[:, gi * POOL_GROUP:(gi + 1) * POOL_GROUP]).astype(o_ref.dtype)


def _pool(u, w, lb):
    B, L, _ = u.shape
    nblk = L // lb
    hb = lb // HALO
    cur = lambda b, i: (b, i, 0)
    prv = lambda b, i: (b, jnp.maximum(i * hb - 1, 0), 0)
    nxt = lambda b, i: (b, jnp.minimum((i + 1) * hb, L // HALO - 1), 0)
    kern = functools.partial(_pool_kernel, nblk=nblk, lb=lb, seq=L)
    return pl.pallas_call(
        kern,
        grid=(B, nblk),
        in_specs=[pl.BlockSpec((1, HALO, POOL_DIM), prv),
                  pl.BlockSpec((1, lb, POOL_DIM), cur),
                  pl.BlockSpec((1, HALO, POOL_DIM), nxt),
                  _const_spec((len(POOL_WINDOWS), POOL_GROUP, POOL_GROUP)),
                  _const_spec((1, POOL_DIM))],
        out_specs=pl.BlockSpec((1, lb, POOL_DIM), cur),
        out_shape=jax.ShapeDtypeStruct((B, L, POOL_DIM), BF16),
        compiler_params=_cparams("parallel", "parallel"),
        name="pool",
    )(u, u, u, w["pool_w"], w["pool_scale"])


def _post_kernel(x_ref, mod_ref, at_ref, sd_ref, po_ref, woa_ref, wos_ref, wop_ref, n2g_ref, wpq_ref,
                 x1_ref, h2_ref, qp_ref):
    g1 = mod_ref[0, 2:3, :]
    sh2 = mod_ref[0, 3:4, :]
    sc2 = mod_ref[0, 4:5, :]
    mix = _dot(at_ref[0], woa_ref[...]) + _dot(sd_ref[0], wos_ref[...]) + _dot(po_ref[0], wop_ref[...])
    x1 = x_ref[0] + g1 * mix
    x1_ref[0] = x1
    h2 = (_rms(x1, n2g_ref[...]) * (1.0 + sc2) + sh2).astype(BF16)
    h2_ref[0] = h2
    qp_ref[0] = _dot(h2, wpq_ref[...]).astype(BF16)


def _post_mixer(x, mod, attn, ssd, pool, w, tm):
    B, L, _ = x.shape
    tok = lambda b, i: (b, i, 0)
    return pl.pallas_call(
        _post_kernel,
        grid=(B, L // tm),
        in_specs=[pl.BlockSpec((1, tm, D_MODEL), tok),
                  pl.BlockSpec((1, 6, D_MODEL), lambda b, i: (b, 0, 0)),
                  pl.BlockSpec((1, tm, ATTN_OUT), tok),
                  pl.BlockSpec((1, tm, SSD_INNER), tok),
                  pl.BlockSpec((1, tm, POOL_DIM), tok),
                  _const_spec((ATTN_OUT, D_MODEL)), _const_spec((SSD_INNER, D_MODEL)),
                  _const_spec((POOL_DIM, D_MODEL)), _const_spec((1, D_MODEL)),
                  _const_spec((D_MODEL, D_MODEL))],
        out_specs=[pl.BlockSpec((1, tm, D_MODEL), tok)] * 3,
        out_shape=[jax.ShapeDtypeStruct((B, L, D_MODEL), F32),
                   jax.ShapeDtypeStruct((B, L, D_MODEL), BF16),
                   jax.ShapeDtypeStruct((B, L, D_MODEL), BF16)],
        compiler_params=_cparams("parallel", "parallel"),
        name="post_mixer",
    )(x, mod, attn, ssd, pool, w["wo_a"], w["wo_s"], w["wo_p"], w["n2g"], w["wpq"])


def _top16(s, out_v, out_i):
    n = s.shape[0]
    rows = lax.broadcasted_iota(I32, s.shape, 0).astype(F32)
    for r in range(PEER_TOPK):
        m = jnp.max(s, axis=0, keepdims=True)
        idx = jnp.min(jnp.where(s == m, rows, float(n)), axis=0, keepdims=True)
        out_v[r:r + 1, :] = m
        out_i[r:r + 1, :] = idx
        s = jnp.where(rows == idx, -jnp.inf, s)


def _topk_kernel(qp_ref, keys_ref, e_ref, g_ref, v1_sc, i1_sc, v2_sc, i2_sc, cv_sc, ci_sc, tv_sc, ti_sc,
                 eo_sc, go_sc):
    cv_sc[...] = jnp.full(cv_sc.shape, -jnp.inf, F32)
    ci_sc[...] = jnp.zeros(ci_sc.shape, F32)
    for hd in range(PEER_HEADS):
        for half, (vs, is_) in enumerate(((v1_sc, i1_sc), (v2_sc, i2_sc))):
            c0 = (hd * 2 + half) * PEER_HALF
            s = _dot_nt(keys_ref[hd * 2 + half], qp_ref[:, c0:c0 + PEER_HALF])
            _top16(s, vs, is_)
        off = 0
        for a, nb in _STAIR:
            cv_sc[off:off + nb, :] = v1_sc[a:a + 1, :] + v2_sc[0:nb, :]
            ci_sc[off:off + nb, :] = i1_sc[a:a + 1, :] * float(N_KEYS) + i2_sc[0:nb, :]
            off += nb
        cand = cv_sc[...]
        cidx = ci_sc[...]
        rows = lax.broadcasted_iota(I32, cand.shape, 0).astype(F32)
        for r in range(PEER_TOPK):
            m = jnp.max(cand, axis=0, keepdims=True)
            pos = jnp.min(jnp.where(cand == m, rows, float(_STAIR_PAD)), axis=0, keepdims=True)
            hit = rows == pos
            tv_sc[r:r + 1, :] = m
            ti_sc[r:r + 1, :] = jnp.sum(jnp.where(hit, cidx, 0.0), axis=0, keepdims=True)
            cand = jnp.where(hit, -jnp.inf, cand)
        tv = tv_sc[...]
        p = jnp.exp(tv - tv[0:1, :])
        go_sc[hd * PEER_TOPK:(hd + 1) * PEER_TOPK, :] = p / jnp.sum(p, axis=0, keepdims=True)
        eo_sc[hd * PEER_TOPK:(hd + 1) * PEER_TOPK, :] = ti_sc[...]
    e_ref[...] = eo_sc[...].T.astype(I32)
    g_ref[...] = go_sc[...].T


def _peer_topk(qp, keys, tm):
    T = qp.shape[0]
    return pl.pallas_call(
        _topk_kernel,
        grid=(T // tm,),
        in_specs=[pl.BlockSpec((tm, D_MODEL), lambda i: (i, 0)),
                  _const_spec((PEER_HEADS * 2, N_KEYS, PEER_HALF))],
        out_specs=[pl.BlockSpec((tm, HK), lambda i: (i, 0))] * 2,
        out_shape=[jax.ShapeDtypeStruct((T, HK), I32), jax.ShapeDtypeStruct((T, HK), F32)],
        scratch_shapes=[pltpu.VMEM((PEER_TOPK, tm), F32)] * 4
                       + [pltpu.VMEM((_STAIR_PAD, tm), F32)] * 2
                       + [pltpu.VMEM((PEER_TOPK, tm), F32)] * 2
                       + [pltpu.VMEM((HK, tm), F32)] * 2,
        compiler_params=_cparams("parallel"),
        name="peer_topk",
    )(qp, keys)


G_PITCH = N_KEYS + 8


G_GROUP = 16


def _gate_kernel(e_ref, g_ref, o_ref, gs_sc, *, tg):
    sub = lax.broadcasted_iota(I32, (N_KEYS, HK), 0)
    ngroup = tg // G_GROUP

    def build(n, slot):
        for u in range(G_GROUP):
            t = n * G_GROUP + u
            er = e_ref[pl.ds(t, 1), :]
            gr = g_ref[pl.ds(t, 1), :]
            i1 = lax.shift_right_logical(er, 7)
            i2 = lax.bitwise_and(er, N_KEYS - 1)
            a = jnp.where(sub == i1, 1.0, 0.0).astype(BF16)
            b = jnp.where(sub == i2, gr, 0.0).astype(BF16)
            gs_sc[slot, pl.ds(u * G_PITCH, N_KEYS), :] = _dot_nt(a, b)

    def regroup(n, slot):
        row = n * G_GROUP if isinstance(n, int) else pl.multiple_of(n * G_GROUP, G_GROUP)
        for c in range(N_KEYS):
            lo = gs_sc[slot, pl.ds(c, 8, stride=G_PITCH), :]
            hi = gs_sc[slot, pl.ds(8 * G_PITCH + c, 8, stride=G_PITCH), :]
            o_ref[pl.ds(row, G_GROUP), c * N_KEYS:(c + 1) * N_KEYS] = (
                jnp.concatenate([lo, hi], axis=0).astype(o_ref.dtype))

    def body(nn, carry):
        n = 2 * nn + 1
        build(n, 1)
        regroup(n - 1, 0)
        build(n + 1, 0)
        regroup(n, 1)
        return carry

    build(0, 0)
    lax.fori_loop(0, (ngroup - 1) // 2, body, 0)
    build(ngroup - 1, 1)
    regroup(ngroup - 2, 0)
    regroup(ngroup - 1, 1)


def _peer_gates(e, g, tg):
    T = e.shape[0]
    assert tg % (2 * G_GROUP) == 0, tg
    kern = functools.partial(_gate_kernel, tg=tg)
    return pl.pallas_call(
        kern,
        grid=(T // tg,),
        in_specs=[pl.BlockSpec((tg, HK), lambda i: (i, 0))] * 2,
        out_specs=pl.BlockSpec((tg, N_EXPERTS), lambda i: (i, 0)),
        out_shape=jax.ShapeDtypeStruct((T, N_EXPERTS), BF16),
        scratch_shapes=[pltpu.VMEM((2, G_GROUP * G_PITCH, N_KEYS), F32)],
        compiler_params=_cparams("parallel"),
        name="peer_gates",
    )(e, g)


def _dense_kernel(h_ref, u_ref, v_ref, gt_ref, o_ref):
    @pl.when(pl.program_id(2) == 0)
    def _():
        o_ref[...] = jnp.zeros(o_ref.shape, F32)

    a = _gelu(_dot_nt(h_ref[0], u_ref[...]))
    wgt = (a * gt_ref[0].astype(F32)).astype(BF16)
    o_ref[0] += _dot(wgt, v_ref[...])


def _peer_dense(h2, u, v, gates, tb, eb):
    B, L, _ = h2.shape
    tok = lambda b, i, j: (b, i, 0)
    return pl.pallas_call(
        _dense_kernel,
        grid=(B, L // tb, N_EXPERTS // eb),
        in_specs=[pl.BlockSpec((1, tb, D_MODEL), tok),
                  pl.BlockSpec((eb, D_MODEL), lambda b, i, j: (j, 0)),
                  pl.BlockSpec((eb, D_MODEL), lambda b, i, j: (j, 0)),
                  pl.BlockSpec((1, tb, eb), lambda b, i, j: (b, i, j))],
        out_specs=pl.BlockSpec((1, tb, D_MODEL), tok),
        out_shape=jax.ShapeDtypeStruct((B, L, D_MODEL), F32),
        compiler_params=_cparams("parallel", "parallel", "arbitrary"),
        name="peer_dense",
    )(h2, u, v, gates)


def _final_kernel(x1_ref, pe_ref, mod_ref, fg_ref, o_ref):
    o_ref[0] = _rms(x1_ref[0] + mod_ref[0, 5:6, :] * pe_ref[0], fg_ref[...])


def _final_norm(x1, peer, mod, fg, tm):
    B, L, _ = x1.shape
    tok = lambda b, i: (b, i, 0)
    return pl.pallas_call(
        _final_kernel,
        grid=(B, L // tm),
        in_specs=[pl.BlockSpec((1, tm, D_MODEL), tok), pl.BlockSpec((1, tm, D_MODEL), tok),
                  pl.BlockSpec((1, 6, D_MODEL), lambda b, i: (b, 0, 0)), _const_spec((1, D_MODEL))],
        out_specs=pl.BlockSpec((1, tm, D_MODEL), tok),
        out_shape=jax.ShapeDtypeStruct((B, L, D_MODEL), F32),
        compiler_params=_cparams("parallel", "parallel"),
        name="final_norm",
    )(x1, peer, mod, fg)


def _rope_tables(L):
    inv = 1.0 / (ROPE_BASE ** (jnp.arange(0, ROPE_DIM, 2, dtype=F32) / ROPE_DIM))
    ang = jnp.arange(L, dtype=F32)[:, None] * inv[None, :]
    cos, sin = jnp.cos(ang), jnp.sin(ang)
    zero = jnp.zeros((L, 128 - ROPE_DIM), F32)
    return jnp.concatenate([cos, cos, zero], axis=1), jnp.concatenate([-sin, sin, zero], axis=1)


def _swap_halves(w):
    half = w.shape[-1] // 2
    return jnp.concatenate([w[..., half:], w[..., :half]], axis=-1)


def _layer_weights(p, i):
    w_in = p["w_in"][i]
    s0, s1, s2, s3, s4, s5 = (Q_LORA, Q_LORA + KV_LORA, Q_LORA + KV_LORA + ROPE_DIM,
                              Q_LORA + KV_LORA + ROPE_DIM + SSD_INNER,
                              Q_LORA + KV_LORA + ROPE_DIM + SSD_INNER + CONV_DIM,
                              Q_LORA + KV_LORA + ROPE_DIM + SSD_INNER + CONV_DIM + 2 * SSD_HEADS)
    w_kr = w_in[:, s1:s2]
    zcol = lambda n: jnp.zeros((D_MODEL, n), F32)
    wcat = jnp.concatenate(
        [w_in[:, :s0], w_in[:, s0:s1], w_in[:, s2:s3], w_in[:, s3:s4], w_in[:, s5:],
         w_kr, w_in[:, s4:s5], zcol(128 - ROPE_DIM - 2 * SSD_HEADS),
         _swap_halves(w_kr), zcol(128 - ROPE_DIM)], axis=1).astype(BF16)
    wqb = p["w_q_b"][i].reshape(Q_LORA, ATTN_HEADS, NOPE_DIM + ROPE_DIM)
    zq = jnp.zeros((Q_LORA, ATTN_HEADS, 128 - ROPE_DIM), F32)
    wq = jnp.concatenate([wqb, zq], axis=2).reshape(Q_LORA, ATTN_HEADS * QK_PAD).astype(BF16)
    wqs = jnp.concatenate([_swap_halves(wqb[:, :, NOPE_DIM:]), zq], axis=2).reshape(
        Q_LORA, ATTN_HEADS * 128).astype(BF16)
    wkv = p["w_kv_b"][i].reshape(KV_LORA, ATTN_HEADS, NOPE_DIM + V_DIM)
    wk = wkv[:, :, :NOPE_DIM].reshape(KV_LORA, ATTN_HEADS * NOPE_DIM).astype(BF16)
    wv = wkv[:, :, NOPE_DIM:].reshape(KV_LORA, ATTN_HEADS * V_DIM).T.astype(BF16)
    lane_pad = lambda f, b: jnp.concatenate(
        [jnp.zeros((DT_LANE,), F32), f, b, jnp.zeros((128 - DT_LANE - 2 * SSD_HEADS,), F32)])[None, :]
    w_out = p["w_out"][i].astype(BF16)
    return dict(
        n1g=p["norm1_g"][i][None, :], n2g=p["norm2_g"][i][None, :],
        wcat=wcat, qg=p["q_a_norm_g"][i][None, :], wq=wq, wqs=wqs,
        kvg=p["kv_a_norm_g"][i][None, :], wk=wk, wv=wv,
        conv_w=p["conv_w"][i], conv_b=p["conv_b"][i][None, :],
        dt_bias=lane_pad(p["dt_bias_fwd"][i], p["dt_bias_bwd"][i]),
        a_neg=lane_pad(-jnp.exp(p["a_log_fwd"][i]), -jnp.exp(p["a_log_bwd"][i])),
        d_skip=jnp.repeat(p["d_skip"][i], SSD_HEAD_DIM)[None, :], ssd_g=p["ssd_norm_g"][i][None, :],
        pool_w=p["pool_w"][i].astype(BF16), pool_scale=p["pool_scale"][i][None, :],
        wo_a=w_out[:ATTN_OUT], wo_s=w_out[ATTN_OUT:ATTN_OUT + SSD_INNER], wo_p=w_out[ATTN_OUT + SSD_INNER:],
        wpq=p["peer_wq"][i].astype(BF16),
        keys=p["peer_keys"][i].reshape(PEER_HEADS * 2, N_KEYS, PEER_HALF).astype(BF16),
        u=p["peer_u"][i].astype(BF16), v=p["peer_v"][i].astype(BF16),
    )


def _block(n, pref):
    for c in pref:
        if n % c == 0:
            return c
    raise ValueError(f"no block size in {pref} divides {n}")


def _encoder(x, mods, weights, fg):
    B, L, _ = x.shape
    cos_t, sin_t = _rope_tables(L)
    tm = _block(L, (256, 128))
    tq = _block(L, (512, 256, 128))
    tk = _block(L // ATTN_SLOTS, (1024, 512, 256, 128))
    nchunk = _block(L // SSD_CHUNK, (4, 2, 1))
    lp = _block(L, (512, 256, 128))
    tb = _block(L, (1024, 512, 256, 128))
    tt = _block(B * L, (512, 256, 128))
    tg = _block(B * L, (64, 32))
    peer = None
    for i in range(DEPTH):
        w, mod = weights[i], mods[i]
        if peer is None:
            q, k, v, z, xbc, pool_in, kd = _pre_mixer(x, mod, w, cos_t, sin_t, tm)
        else:
            q, k, v, z, xbc, pool_in, kd, x = _pre_mixer(x1, mod, w, cos_t, sin_t, tm, peer, mods[i - 1])
        attn = _attention(q, k, v, tq, tk)
        yf = _ssd_pass(xbc, kd, w, nchunk, reverse=False)
        ssd = _ssd_pass(xbc, kd, w, nchunk, reverse=True, yf=yf, z=z)
        pool = _pool(pool_in, w, lp)
        x1, h2, qp = _post_mixer(x, mod, attn, ssd, pool, w, tm)
        e, g = _peer_topk(qp.reshape(B * L, D_MODEL), w["keys"], tt)
        gates = _peer_gates(e, g, tg).reshape(B, L, N_EXPERTS)
        peer = _peer_dense(h2, w["u"], w["v"], gates, tb, 1024)
    return _final_norm(x1, peer, mods[DEPTH - 1], fg, tm)


def kernel(x_prompt, x_sample, c_prompt, c_sample, mod_w, mod_b, norm1_g, norm2_g, w_in, q_a_norm_g, w_q_b, kv_a_norm_g, w_kv_b, conv_w, conv_b, a_log_fwd, a_log_bwd, dt_bias_fwd, dt_bias_bwd, d_skip, ssd_norm_g, pool_w, pool_scale, w_out, peer_wq, peer_keys, peer_u, peer_v, final_norm_g):
    p = dict(mod_w=mod_w, mod_b=mod_b, norm1_g=norm1_g, norm2_g=norm2_g, w_in=w_in,
             q_a_norm_g=q_a_norm_g, w_q_b=w_q_b, kv_a_norm_g=kv_a_norm_g, w_kv_b=w_kv_b,
             conv_w=conv_w, conv_b=conv_b, a_log_fwd=a_log_fwd, a_log_bwd=a_log_bwd,
             dt_bias_fwd=dt_bias_fwd, dt_bias_bwd=dt_bias_bwd, d_skip=d_skip, ssd_norm_g=ssd_norm_g,
             pool_w=pool_w, pool_scale=pool_scale, w_out=w_out, peer_wq=peer_wq, peer_keys=peer_keys,
             peer_u=peer_u, peer_v=peer_v)
    weights = [_layer_weights(p, i) for i in range(DEPTH)]
    bp, bs = c_prompt.shape[0], c_sample.shape[0]
    c_pad = jnp.concatenate([c_prompt, c_sample, jnp.zeros((8 - bp - bs, D_MODEL), F32)], axis=0)
    mods = [_modulation(c_pad, mod_w[i], mod_b[i][None, :]) for i in range(DEPTH)]
    fg = final_norm_g[None, :]
    mods_p = [m[:bp].reshape(bp, 6, D_MODEL) for m in mods]
    mods_s = [m[bp:bp + bs].reshape(bs, 6, D_MODEL) for m in mods]
    return (_encoder(x_prompt, mods_p, weights, fg), _encoder(x_sample, mods_s, weights, fg))
```

```python
import functools
import math

import jax
import jax.numpy as jnp
from jax import lax
from jax.experimental import pallas as pl
from jax.experimental.pallas import tpu as pltpu

F32 = jnp.float32
BF16 = jnp.bfloat16
I32 = jnp.int32

D_MODEL = 2048
DEPTH = 2
EPS = 1e-6
ATTN_HEADS = 8
Q_LORA = 512
KV_LORA = 256
NOPE_DIM = 128
ROPE_DIM = 64
V_DIM = 128
ROPE_BASE = 10000.0
QK_PAD = 256
SSD_HEADS = 8
SSD_HEAD_DIM = 64
SSD_INNER = SSD_HEADS * SSD_HEAD_DIM
SSD_GROUPS = 2
SSD_STATE = 128
SSD_CHUNK = 128
D_CONV = 4
CONV_DIM = SSD_INNER + 2 * SSD_GROUPS * SSD_STATE
POOL_WINDOWS = (2, 4, 8, 16)
POOL_GROUP = 128
POOL_DIM = len(POOL_WINDOWS) * POOL_GROUP
ATTN_OUT = ATTN_HEADS * V_DIM
PEER_HEADS = 8
N_KEYS = 128
N_EXPERTS = N_KEYS * N_KEYS
PEER_HALF = 128
PEER_TOPK = 16
HK = PEER_HEADS * PEER_TOPK
HALO = 8
DT_LANE = 64

C_CQ = 0
C_CKV = C_CQ + Q_LORA
C_Z = C_CKV + KV_LORA
C_XBC = C_Z + SSD_INNER
C_POOL = C_XBC + CONV_DIM
C_KD = C_POOL + POOL_DIM
C_KDS = C_KD + 128
C_END = C_KDS + 128

VMEM_LIMIT = 56 * 1024 * 1024

_STAIR = [(a, PEER_TOPK // (a + 1)) for a in range(PEER_TOPK)]
_STAIR_ROWS = sum(n for _, n in _STAIR)
_STAIR_PAD = ((_STAIR_ROWS + 7) // 8) * 8


def _cparams(*sem):
    return pltpu.CompilerParams(dimension_semantics=sem, vmem_limit_bytes=VMEM_LIMIT)


def _const_spec(shape):
    nd = len(shape)
    return pl.BlockSpec(shape, lambda *_: (0,) * nd, pipeline_mode=pl.Buffered(1))


def _rms(x, g):
    return x * lax.rsqrt(jnp.mean(x * x, axis=-1, keepdims=True) + EPS) * g


def _silu(x):
    return x * jax.nn.sigmoid(x)


def _gelu(x):
    return 0.5 * x * (1.0 + lax.erf(x * (1.0 / math.sqrt(2.0))))


def _dot(a, b):
    return jnp.dot(a, b, preferred_element_type=F32)


def _dot_nt(a, b):
    return lax.dot_general(a, b, (((1,), (1,)), ((), ())), preferred_element_type=F32)


def _mod_kernel(c_ref, w_ref, b_ref, o_ref):
    cs = _silu(c_ref[...])
    o_ref[...] = _dot(cs.astype(BF16), w_ref[0].astype(BF16)) + b_ref[...]


def _modulation(c_pad, mod_w, mod_b, layer):
    bn = 1024
    n = mod_w.shape[2]
    return pl.pallas_call(
        _mod_kernel,
        grid=(n // bn,),
        in_specs=[pl.BlockSpec((8, D_MODEL), lambda j: (0, 0)),
                  pl.BlockSpec((1, D_MODEL, bn), lambda j: (layer, 0, j)),
                  pl.BlockSpec((1, bn), lambda j: (0, j))],
        out_specs=pl.BlockSpec((8, bn), lambda j: (0, j)),
        out_shape=jax.ShapeDtypeStruct((8, n), F32),
        compiler_params=_cparams("parallel"),
        name="modulation",
    )(c_pad, mod_w, mod_b)


def _pre_kernel(*refs, residual):
    if residual:
        pe_ref, pmod_ref, x_ref, *refs = refs
    else:
        x_ref, *refs = refs
    (mod_ref, n1g_ref, wcat_ref, qg_ref, wq_ref, wqs_ref, kvg_ref, wk_ref, wv_ref, cos_ref, sin_ref,
     q_ref, k_ref, v_ref, z_ref, xbc_ref, pool_ref, kd_ref, *xo_ref) = refs
    x = x_ref[0]
    if residual:
        x = x + pmod_ref[0, 5:6, :] * pe_ref[0]
        xo_ref[0][0] = x
    sh1 = mod_ref[0, 0:1, :]
    sc1 = mod_ref[0, 1:2, :]
    h = _rms(x, n1g_ref[...]) * (1.0 + sc1) + sh1
    proj = _dot(h.astype(BF16), wcat_ref[...])
    z_ref[0] = proj[:, C_Z:C_XBC]
    xbc_ref[0] = proj[:, C_XBC:C_POOL]
    pool_ref[0] = proj[:, C_POOL:C_KD]
    kd = proj[:, C_KD:C_KDS]
    kd_ref[0] = kd
    cos = cos_ref[...]
    sin = sin_ref[...]
    scale = math.log2(math.e) / math.sqrt(NOPE_DIM + ROPE_DIM)

    cqn = _rms(proj[:, C_CQ:C_CKV], qg_ref[...]).astype(BF16)
    qm = _dot(cqn, wq_ref[...])
    qs = _dot(cqn, wqs_ref[...])
    for hd in range(ATTN_HEADS):
        o = hd * QK_PAD
        q_ref[0, hd, :, 0:NOPE_DIM] = (qm[:, o:o + NOPE_DIM] * scale).astype(BF16)
        rope = qm[:, o + NOPE_DIM:o + QK_PAD] * cos + qs[:, hd * 128:(hd + 1) * 128] * sin
        q_ref[0, hd, :, NOPE_DIM:QK_PAD] = (rope * scale).astype(BF16)

    ckvn = _rms(proj[:, C_CKV:C_Z], kvg_ref[...]).astype(BF16)
    kn = _dot(ckvn, wk_ref[...])
    vvt = _dot_nt(wv_ref[...], ckvn)
    krope = (kd * cos + proj[:, C_KDS:C_END] * sin).astype(BF16)
    for hd in range(ATTN_HEADS):
        k_ref[0, hd, :, 0:NOPE_DIM] = kn[:, hd * NOPE_DIM:(hd + 1) * NOPE_DIM].astype(BF16)
        k_ref[0, hd, :, NOPE_DIM:QK_PAD] = krope
        v_ref[0, hd] = vvt[hd * V_DIM:(hd + 1) * V_DIM, :].astype(BF16)


def _pre_mixer(x, mod, w, cos_t, sin_t, tm, peer=None, prev_mod=None):
    B, L, _ = x.shape
    grid = (B, L // tm)
    tok = lambda b, i: (b, i, 0)
    head = lambda b, i: (b, 0, i, 0)
    modspec = pl.BlockSpec((1, 6, D_MODEL), lambda b, i: (b, 0, 0))
    residual = peer is not None
    res_specs = [pl.BlockSpec((1, tm, D_MODEL), tok), modspec] if residual else []
    res_args = [peer, prev_mod] if residual else []
    res_out_specs = [pl.BlockSpec((1, tm, D_MODEL), tok)] if residual else []
    res_out_shape = [jax.ShapeDtypeStruct((B, L, D_MODEL), F32)] if residual else []
    return pl.pallas_call(
        functools.partial(_pre_kernel, residual=residual),
        grid=grid,
        in_specs=res_specs + [
                  pl.BlockSpec((1, tm, D_MODEL), tok),
                  modspec,
                  _const_spec((1, D_MODEL)),
                  _const_spec((D_MODEL, C_END)),
                  _const_spec((1, Q_LORA)),
                  _const_spec((Q_LORA, ATTN_HEADS * QK_PAD)),
                  _const_spec((Q_LORA, ATTN_HEADS * 128)),
                  _const_spec((1, KV_LORA)),
                  _const_spec((KV_LORA, ATTN_HEADS * NOPE_DIM)),
                  _const_spec((ATTN_HEADS * V_DIM, KV_LORA)),
                  pl.BlockSpec((tm, 128), lambda b, i: (i, 0)),
                  pl.BlockSpec((tm, 128), lambda b, i: (i, 0))],
        out_specs=[pl.BlockSpec((1, ATTN_HEADS, tm, QK_PAD), head),
                   pl.BlockSpec((1, ATTN_HEADS, tm, QK_PAD), head),
                   pl.BlockSpec((1, ATTN_HEADS, V_DIM, tm), lambda b, i: (b, 0, 0, i)),
                   pl.BlockSpec((1, tm, SSD_INNER), tok),
                   pl.BlockSpec((1, tm, CONV_DIM), tok),
                   pl.BlockSpec((1, tm, POOL_DIM), tok),
                   pl.BlockSpec((1, tm, 128), tok)] + res_out_specs,
        out_shape=[jax.ShapeDtypeStruct((B, ATTN_HEADS, L, QK_PAD), BF16),
                   jax.ShapeDtypeStruct((B, ATTN_HEADS, L, QK_PAD), BF16),
                   jax.ShapeDtypeStruct((B, ATTN_HEADS, V_DIM, L), BF16),
                   jax.ShapeDtypeStruct((B, L, SSD_INNER), F32),
                   jax.ShapeDtypeStruct((B, L, CONV_DIM), F32),
                   jax.ShapeDtypeStruct((B, L, POOL_DIM), F32),
                   jax.ShapeDtypeStruct((B, L, 128), F32)] + res_out_shape,
        compiler_params=_cparams("parallel", "parallel"),
        name="pre_mixer",
    )(*res_args, x, mod, w["n1g"], w["wcat"], w["qg"], w["wq"], w["wqs"], w["kvg"], w["wk"], w["wv"],
      cos_t, sin_t)


ATTN_SLOTS = 4


def _attn_kernel(q_ref, k_ref, vt_ref, o_ref, st_sc, p_sc, *, tk, nk):
    q = q_ref[0, 0]
    tq = q.shape[0]
    ntrip = nk // ATTN_SLOTS

    def scores(j):
        st = _dot_nt(k_ref[0, 0, j * tk:(j + 1) * tk, :], q)
        return st, jnp.max(st, axis=0, keepdims=True)

    def weighted_values(j, slot):
        return _dot(vt_ref[0, 0, :, j * tk:(j + 1) * tk], p_sc[slot])

    def trip(jj, carry, first, last):
        for u in range(ATTN_SLOTS):
            j = ATTN_SLOTS * jj + u
            m_prev, l_prev, acc, a_prev, cmax, cmax_1 = carry
            if not (first and u == 0):
                acc = a_prev * acc + weighted_values(j - 1, (u - 1) % 2)
            if last and u >= ATTN_SLOTS - 2:
                cmax_2 = cmax_1
            else:
                st, cmax_2 = scores(j + 2)
                st_sc[(u + 2) % ATTN_SLOTS] = st
            m_new = jnp.maximum(m_prev, cmax)
            alpha = jnp.exp2(m_prev - m_new)
            p = jnp.exp2(st_sc[u] - m_new)
            l_new = alpha * l_prev + jnp.sum(p, axis=0, keepdims=True)
            p_sc[u % 2] = p.astype(BF16)
            carry = (m_new, l_new, acc, alpha, cmax_1, cmax_2)
        return carry

    st_sc[0], cmax0 = scores(0)
    st_sc[1], cmax1 = scores(1)
    carry = (jnp.full((1, tq), -jnp.inf, F32), jnp.zeros((1, tq), F32), jnp.zeros((V_DIM, tq), F32),
             jnp.ones((1, tq), F32), cmax0, cmax1)
    carry = trip(0, carry, True, ntrip == 1)
    for jj in range(1, ntrip - 1):
        carry = trip(jj, carry, False, False)
    if ntrip > 1:
        carry = trip(ntrip - 1, carry, False, True)
    _, l, acc, a_last, _, _ = carry
    acc = a_last * acc + weighted_values(nk - 1, (ATTN_SLOTS - 1) % 2)
    o_ref[0] = (acc / l).T.astype(o_ref.dtype)


def _attention(q, k, vt, tq, tk):
    B, H, L, _ = q.shape
    nk = L // tk
    assert nk % ATTN_SLOTS == 0, (L, tk)
    kern = functools.partial(_attn_kernel, tk=tk, nk=nk)
    return pl.pallas_call(
        kern,
        grid=(B, H, L // tq),
        in_specs=[pl.BlockSpec((1, 1, tq, QK_PAD), lambda b, h, i: (b, h, i, 0)),
                  pl.BlockSpec((1, 1, L, QK_PAD), lambda b, h, i: (b, h, 0, 0)),
                  pl.BlockSpec((1, 1, V_DIM, L), lambda b, h, i: (b, h, 0, 0))],
        out_specs=pl.BlockSpec((1, tq, V_DIM), lambda b, h, i: (b, i, h)),
        out_shape=jax.ShapeDtypeStruct((B, L, H * V_DIM), BF16),
        scratch_shapes=[pltpu.VMEM((ATTN_SLOTS, tk, tq), F32), pltpu.VMEM((2, tk, tq), BF16)],
        compiler_params=_cparams("parallel", "parallel", "arbitrary"),
        name="attention",
    )(q, k, vt)


def _halo_ext(prev_ref, cur_ref, next_ref, i, nblk):
    prev = jnp.where(i > 0, prev_ref[0], 0.0)
    nxt = jnp.where(i < nblk - 1, next_ref[0], 0.0)
    return jnp.concatenate([prev, cur_ref[0], nxt], axis=0)


def _shift_rows(ext, d, lb):
    n = ext.shape[0]
    r = ext if d == 0 else pltpu.roll(ext, (-d) % n, 0)
    return r[HALO:HALO + lb]


def _ssd_kernel(*refs, reverse, final, nblk, nchunk):
    if final:
        (xp_ref, xc_ref, xn_ref, kd_ref, cw_ref, cb_ref, dtb_ref, a_ref, yf_ref, z_ref, dsk_ref, ng_ref,
         o_ref, st_sc) = refs
    else:
        xp_ref, xc_ref, xn_ref, kd_ref, cw_ref, cb_ref, dtb_ref, a_ref, o_ref, st_sc = refs
    step = pl.program_id(1)
    blk = (nblk - 1 - step) if reverse else step
    lb = nchunk * SSD_CHUNK

    @pl.when(step == 0)
    def _():
        st_sc[...] = jnp.zeros(st_sc.shape, F32)

    ext = _halo_ext(xp_ref, xc_ref, xn_ref, blk, nblk)
    conv = cb_ref[...] + sum(cw_ref[kk:kk + 1, :] * _shift_rows(ext, kk - D_CONV // 2, lb) for kk in range(D_CONV))
    act = _silu(conv)
    dt_all = jax.nn.softplus(kd_ref[0] + dtb_ref[...])
    da_all = dt_all * a_ref[...]

    row = lax.broadcasted_iota(I32, (SSD_CHUNK, SSD_CHUNK), 0)
    col = lax.broadcasted_iota(I32, (SSD_CHUNK, SSD_CHUNK), 1)
    keep = (row <= col) if reverse else (row >= col)
    tri = jnp.where(keep, 1.0, 0.0).astype(F32)
    lane = lax.broadcasted_iota(I32, (SSD_CHUNK, 128), 1)
    low = lane < SSD_HEAD_DIM
    dlane = DT_LANE + (SSD_HEADS if reverse else 0)
    edge = 0 if reverse else SSD_CHUNK - 1

    def pair(arr, h0):
        return jnp.where(low, arr[:, dlane + h0:dlane + h0 + 1], arr[:, dlane + h0 + 1:dlane + h0 + 2])

    ys = []
    for cc in (range(nchunk - 1, -1, -1) if reverse else range(nchunk)):
        r0 = cc * SSD_CHUNK
        da = da_all[r0:r0 + SSD_CHUNK]
        dt = dt_all[r0:r0 + SSD_CHUNK]
        acum = jnp.dot(tri, da, preferred_element_type=F32, precision=lax.Precision.HIGHEST)
        acum_t = acum.T
        a_edge = acum[edge:edge + 1, :]
        ycols = [None] * (SSD_HEADS // 2)
        for g in range(SSD_GROUPS):
            bg = act[r0:r0 + SSD_CHUNK, SSD_INNER + g * SSD_STATE:SSD_INNER + (g + 1) * SSD_STATE]
            cg = act[r0:r0 + SSD_CHUNK, SSD_INNER + (SSD_GROUPS + g) * SSD_STATE:
                     SSD_INNER + (SSD_GROUPS + g + 1) * SSD_STATE]
            cgb = cg.astype(BF16)
            cb = _dot_nt(cgb, bg.astype(BF16))
            bgt = bg.T.astype(BF16)
            for pp in range(SSD_HEADS // SSD_GROUPS // 2):
                pi = g * 2 + pp
                h0 = 2 * pi
                xdt = act[r0:r0 + SSD_CHUNK, pi * 128:(pi + 1) * 128] * pair(dt, h0)
                xdtb = xdt.astype(BF16)
                yd = []
                for hh in (h0, h0 + 1):
                    seg = acum[:, dlane + hh:dlane + hh + 1] - acum_t[dlane + hh:dlane + hh + 1, :]
                    decay = jnp.exp(jnp.where(keep, seg, -jnp.inf))
                    yd.append(_dot((cb * decay).astype(BF16), xdtb))
                y_diag = jnp.where(low, yd[0], yd[1])
                st = st_sc[pi]
                y_off = _dot(cgb, st.astype(BF16)) * jnp.exp(pair(acum, h0))
                ycols[pi] = y_diag + y_off
                to_edge = jnp.exp(pair(a_edge - acum, h0))
                st_sc[pi] = st * jnp.exp(pair(a_edge, h0)) + _dot(bgt, (xdt * to_edge).astype(BF16))
        ys.append((cc, jnp.concatenate(ycols, axis=1)))
    y = jnp.concatenate([v for _, v in sorted(ys, key=lambda t: t[0])], axis=0)

    if final:
        tot = yf_ref[0] + y + act[:, 0:SSD_INNER] * dsk_ref[...]
        o_ref[0] = _rms(tot * _silu(z_ref[0]), ng_ref[...]).astype(o_ref.dtype)
    else:
        o_ref[0] = y


def _ssd_pass(xbc, kd, w, nchunk, reverse, yf=None, z=None):
    B, L, _ = xbc.shape
    lb = nchunk * SSD_CHUNK
    nblk = L // lb
    hb = lb // HALO
    final = yf is not None
    bi = (lambda i: nblk - 1 - i) if reverse else (lambda i: i)
    cur = lambda b, i: (b, bi(i), 0)
    prv = lambda b, i: (b, jnp.maximum(bi(i) * hb - 1, 0), 0)
    nxt = lambda b, i: (b, jnp.minimum((bi(i) + 1) * hb, L // HALO - 1), 0)
    in_specs = [pl.BlockSpec((1, HALO, CONV_DIM), prv),
                pl.BlockSpec((1, lb, CONV_DIM), cur),
                pl.BlockSpec((1, HALO, CONV_DIM), nxt),
                pl.BlockSpec((1, lb, 128), cur),
                _const_spec((D_CONV, CONV_DIM)), _const_spec((1, CONV_DIM)),
                _const_spec((1, 128)), _const_spec((1, 128))]
    args = [xbc, xbc, xbc, kd, w["conv_w"], w["conv_b"], w["dt_bias"], w["a_neg"]]
    if final:
        in_specs += [pl.BlockSpec((1, lb, SSD_INNER), cur), pl.BlockSpec((1, lb, SSD_INNER), cur),
                     _const_spec((1, SSD_INNER)), _const_spec((1, SSD_INNER))]
        args += [yf, z, w["d_skip"], w["ssd_g"]]
    kern = functools.partial(_ssd_kernel, reverse=reverse, final=final, nblk=nblk, nchunk=nchunk)
    return pl.pallas_call(
        kern,
        grid=(B, nblk),
        in_specs=in_specs,
        out_specs=pl.BlockSpec((1, lb, SSD_INNER), cur),
        out_shape=jax.ShapeDtypeStruct((B, L, SSD_INNER), BF16 if final else F32),
        scratch_shapes=[pltpu.VMEM((SSD_HEADS // 2, SSD_STATE, 128), F32)],
        compiler_params=_cparams("parallel", "arbitrary"),
        name="ssd_bwd" if reverse else "ssd_fwd",
    )(*args)


def _pool_kernel(up_ref, uc_ref, un_ref, pw_ref, ps_ref, o_ref, *, nblk, lb, seq):
    blk = pl.program_id(1)
    ext = _halo_ext(up_ref, uc_ref, un_ref, blk, nblk)
    n = ext.shape[0]
    t = blk * lb + lax.broadcasted_iota(I32, (lb, 1), 0)
    for gi, wdw in enumerate(POOL_WINDOWS):
        e = ext[:, gi * POOL_GROUP:(gi + 1) * POOL_GROUP]
        run, width = e, 1
        while width < wdw:
            run = run + pltpu.roll(run, width, 0)
            width *= 2
        tot = _shift_rows(run, wdw // 2 - 1, lb)
        cnt = jnp.minimum(t + wdw // 2, seq) - jnp.maximum(t - wdw // 2, 0)
        d = tot / cnt.astype(F32) - e[HALO:HALO + lb]
        y = _dot(d.astype(BF16), pw_ref[gi])
        o_ref[0, :, gi * POOL_GROUP:(gi + 1) * POOL_GROUP] = (
            y * ps_ref[:, gi * POOL_GROUP:(gi + 1) * POOL_GROUP]).astype(o_ref.dtype)


def _pool(u, w, lb):
    B, L, _ = u.shape
    nblk = L // lb
    hb = lb // HALO
    cur = lambda b, i: (b, i, 0)
    prv = lambda b, i: (b, jnp.maximum(i * hb - 1, 0), 0)
    nxt = lambda b, i: (b, jnp.minimum((i + 1) * hb, L // HALO - 1), 0)
    kern = functools.partial(_pool_kernel, nblk=nblk, lb=lb, seq=L)
    return pl.pallas_call(
        kern,
        grid=(B, nblk),
        in_specs=[pl.BlockSpec((1, HALO, POOL_DIM), prv),
                  pl.BlockSpec((1, lb, POOL_DIM), cur),
                  pl.BlockSpec((1, HALO, POOL_DIM), nxt),
                  _const_spec((len(POOL_WINDOWS), POOL_GROUP, POOL_GROUP)),
                  _const_spec((1, POOL_DIM))],
        out_specs=pl.BlockSpec((1, lb, POOL_DIM), cur),
        out_shape=jax.ShapeDtypeStruct((B, L, POOL_DIM), BF16),
        compiler_params=_cparams("parallel", "parallel"),
        name="pool",
    )(u, u, u, w["pool_w"], w["pool_scale"])


def _post_kernel(x_ref, mod_ref, at_ref, sd_ref, po_ref, woa_ref, wos_ref, wop_ref, n2g_ref, wpq_ref,
                 x1_ref, h2_ref, qp_ref):
    g1 = mod_ref[0, 2:3, :]
    sh2 = mod_ref[0, 3:4, :]
    sc2 = mod_ref[0, 4:5, :]
    mix = _dot(at_ref[0], woa_ref[...]) + _dot(sd_ref[0], wos_ref[...]) + _dot(po_ref[0], wop_ref[...])
    x1 = x_ref[0] + g1 * mix
    x1_ref[0] = x1
    h2 = (_rms(x1, n2g_ref[...]) * (1.0 + sc2) + sh2).astype(BF16)
    h2_ref[0] = h2
    qp_ref[0] = _dot(h2, wpq_ref[...]).astype(BF16)


def _post_mixer(x, mod, attn, ssd, pool, w, tm):
    B, L, _ = x.shape
    tok = lambda b, i: (b, i, 0)
    return pl.pallas_call(
        _post_kernel,
        grid=(B, L // tm),
        in_specs=[pl.BlockSpec((1, tm, D_MODEL), tok),
                  pl.BlockSpec((1, 6, D_MODEL), lambda b, i: (b, 0, 0)),
                  pl.BlockSpec((1, tm, ATTN_OUT), tok),
                  pl.BlockSpec((1, tm, SSD_INNER), tok),
                  pl.BlockSpec((1, tm, POOL_DIM), tok),
                  _const_spec((ATTN_OUT, D_MODEL)), _const_spec((SSD_INNER, D_MODEL)),
                  _const_spec((POOL_DIM, D_MODEL)), _const_spec((1, D_MODEL)),
                  _const_spec((D_MODEL, D_MODEL))],
        out_specs=[pl.BlockSpec((1, tm, D_MODEL), tok)] * 3,
        out_shape=[jax.ShapeDtypeStruct((B, L, D_MODEL), F32),
                   jax.ShapeDtypeStruct((B, L, D_MODEL), BF16),
                   jax.ShapeDtypeStruct((B, L, D_MODEL), BF16)],
        compiler_params=_cparams("parallel", "parallel"),
        name="post_mixer",
    )(x, mod, attn, ssd, pool, w["wo_a"], w["wo_s"], w["wo_p"], w["n2g"], w["wpq"])


def _top16(s, out_v, out_i):
    n = s.shape[0]
    rows = lax.broadcasted_iota(I32, s.shape, 0).astype(F32)
    for r in range(PEER_TOPK):
        m = jnp.max(s, axis=0, keepdims=True)
        idx = jnp.min(jnp.where(s == m, rows, float(n)), axis=0, keepdims=True)
        out_v[r:r + 1, :] = m
        out_i[r:r + 1, :] = idx
        s = jnp.where(rows == idx, -jnp.inf, s)


def _topk_kernel(qp_ref, keys_ref, e_ref, g_ref, v1_sc, i1_sc, v2_sc, i2_sc, cv_sc, ci_sc, tv_sc, ti_sc,
                 eo_sc, go_sc):
    cv_sc[...] = jnp.full(cv_sc.shape, -jnp.inf, F32)
    ci_sc[...] = jnp.zeros(ci_sc.shape, F32)
    for hd in range(PEER_HEADS):
        for half, (vs, is_) in enumerate(((v1_sc, i1_sc), (v2_sc, i2_sc))):
            c0 = (hd * 2 + half) * PEER_HALF
            s = _dot_nt(keys_ref[hd * 2 + half], qp_ref[:, c0:c0 + PEER_HALF])
            _top16(s, vs, is_)
        off = 0
        for a, nb in _STAIR:
            cv_sc[off:off + nb, :] = v1_sc[a:a + 1, :] + v2_sc[0:nb, :]
            ci_sc[off:off + nb, :] = i1_sc[a:a + 1, :] * float(N_KEYS) + i2_sc[0:nb, :]
            off += nb
        cand = cv_sc[...]
        cidx = ci_sc[...]
        rows = lax.broadcasted_iota(I32, cand.shape, 0).astype(F32)
        for r in range(PEER_TOPK):
            m = jnp.max(cand, axis=0, keepdims=True)
            pos = jnp.min(jnp.where(cand == m, rows, float(_STAIR_PAD)), axis=0, keepdims=True)
            hit = rows == pos
            tv_sc[r:r + 1, :] = m
            ti_sc[r:r + 1, :] = jnp.sum(jnp.where(hit, cidx, 0.0), axis=0, keepdims=True)
            cand = jnp.where(hit, -jnp.inf, cand)
        tv = tv_sc[...]
        p = jnp.exp(tv - tv[0:1, :])
        go_sc[hd * PEER_TOPK:(hd + 1) * PEER_TOPK, :] = p / jnp.sum(p, axis=0, keepdims=True)
        eo_sc[hd * PEER_TOPK:(hd + 1) * PEER_TOPK, :] = ti_sc[...]
    e_ref[...] = eo_sc[...].T.astype(I32)
    g_ref[...] = go_sc[...].T


def _peer_topk(qp, keys, tm):
    T = qp.shape[0]
    return pl.pallas_call(
        _topk_kernel,
        grid=(T // tm,),
        in_specs=[pl.BlockSpec((tm, D_MODEL), lambda i: (i, 0)),
                  _const_spec((PEER_HEADS * 2, N_KEYS, PEER_HALF))],
        out_specs=[pl.BlockSpec((tm, HK), lambda i: (i, 0))] * 2,
        out_shape=[jax.ShapeDtypeStruct((T, HK), I32), jax.ShapeDtypeStruct((T, HK), F32)],
        scratch_shapes=[pltpu.VMEM((PEER_TOPK, tm), F32)] * 4
                       + [pltpu.VMEM((_STAIR_PAD, tm), F32)] * 2
                       + [pltpu.VMEM((PEER_TOPK, tm), F32)] * 2
                       + [pltpu.VMEM((HK, tm), F32)] * 2,
        compiler_params=_cparams("parallel"),
        name="peer_topk",
    )(qp, keys)


G_PITCH = N_KEYS + 8


G_GROUP = 16


def _gate_kernel(e_ref, g_ref, o_ref, gs_sc, *, tg):
    sub = lax.broadcasted_iota(I32, (N_KEYS, HK), 0)
    ngroup = tg // G_GROUP

    def build(n, slot):
        for u in range(G_GROUP):
            t = n * G_GROUP + u
            er = e_ref[pl.ds(t, 1), :]
            gr = g_ref[pl.ds(t, 1), :]
            i1 = lax.shift_right_logical(er, 7)
            i2 = lax.bitwise_and(er, N_KEYS - 1)
            a = jnp.where(sub == i1, 1.0, 0.0).astype(BF16)
            b = jnp.where(sub == i2, gr, 0.0).astype(BF16)
            gs_sc[slot, pl.ds(u * G_PITCH, N_KEYS), :] = _dot_nt(a, b)

    def regroup(n, slot):
        row = n * G_GROUP if isinstance(n, int) else pl.multiple_of(n * G_GROUP, G_GROUP)
        for c in range(N_KEYS):
            lo = gs_sc[slot, pl.ds(c, 8, stride=G_PITCH), :]
            hi = gs_sc[slot, pl.ds(8 * G_PITCH + c, 8, stride=G_PITCH), :]
            o_ref[pl.ds(row, G_GROUP), c * N_KEYS:(c + 1) * N_KEYS] = (
                jnp.concatenate([lo, hi], axis=0).astype(o_ref.dtype))

    def body(nn, carry):
        n = 2 * nn + 1
        build(n, 1)
        regroup(n - 1, 0)
        build(n + 1, 0)
        regroup(n, 1)
        return carry

    build(0, 0)
    lax.fori_loop(0, (ngroup - 1) // 2, body, 0)
    build(ngroup - 1, 1)
    regroup(ngroup - 2, 0)
    regroup(ngroup - 1, 1)


def _peer_gates(e, g, tg):
    T = e.shape[0]
    assert tg % (2 * G_GROUP) == 0, tg
    kern = functools.partial(_gate_kernel, tg=tg)
    return pl.pallas_call(
        kern,
        grid=(T // tg,),
        in_specs=[pl.BlockSpec((tg, HK), lambda i: (i, 0))] * 2,
        out_specs=pl.BlockSpec((tg, N_EXPERTS), lambda i: (i, 0)),
        out_shape=jax.ShapeDtypeStruct((T, N_EXPERTS), BF16),
        scratch_shapes=[pltpu.VMEM((2, G_GROUP * G_PITCH, N_KEYS), F32)],
        compiler_params=_cparams("parallel"),
        name="peer_gates",
    )(e, g)


def _dense_kernel(h_ref, u_ref, v_ref, gt_ref, o_ref):
    @pl.when(pl.program_id(2) == 0)
    def _():
        o_ref[...] = jnp.zeros(o_ref.shape, F32)

    a = _gelu(_dot_nt(h_ref[0], u_ref[0]))
    wgt = (a * gt_ref[0].astype(F32)).astype(BF16)
    o_ref[0] += _dot(wgt, v_ref[0])


def _peer_dense(h2, u, v, gates, tb, eb, layer):
    B, L, _ = h2.shape
    tok = lambda b, i, j: (b, i, 0)
    return pl.pallas_call(
        _dense_kernel,
        grid=(B, L // tb, N_EXPERTS // eb),
        in_specs=[pl.BlockSpec((1, tb, D_MODEL), tok),
                  pl.BlockSpec((1, eb, D_MODEL), lambda b, i, j: (layer, j, 0)),
                  pl.BlockSpec((1, eb, D_MODEL), lambda b, i, j: (layer, j, 0)),
                  pl.BlockSpec((1, tb, eb), lambda b, i, j: (b, i, j))],
        out_specs=pl.BlockSpec((1, tb, D_MODEL), tok),
        out_shape=jax.ShapeDtypeStruct((B, L, D_MODEL), F32),
        compiler_params=_cparams("parallel", "parallel", "arbitrary"),
        name="peer_dense",
    )(h2, u, v, gates)


def _final_kernel(x1_ref, pe_ref, mod_ref, fg_ref, o_ref):
    o_ref[0] = _rms(x1_ref[0] + mod_ref[0, 5:6, :] * pe_ref[0], fg_ref[...])


def _final_norm(x1, peer, mod, fg, tm):
    B, L, _ = x1.shape
    tok = lambda b, i: (b, i, 0)
    return pl.pallas_call(
        _final_kernel,
        grid=(B, L // tm),
        in_specs=[pl.BlockSpec((1, tm, D_MODEL), tok), pl.BlockSpec((1, tm, D_MODEL), tok),
                  pl.BlockSpec((1, 6, D_MODEL), lambda b, i: (b, 0, 0)), _const_spec((1, D_MODEL))],
        out_specs=pl.BlockSpec((1, tm, D_MODEL), tok),
        out_shape=jax.ShapeDtypeStruct((B, L, D_MODEL), F32),
        compiler_params=_cparams("parallel", "parallel"),
        name="final_norm",
    )(x1, peer, mod, fg)


def _rope_tables(L):
    inv = 1.0 / (ROPE_BASE ** (jnp.arange(0, ROPE_DIM, 2, dtype=F32) / ROPE_DIM))
    ang = jnp.arange(L, dtype=F32)[:, None] * inv[None, :]
    cos, sin = jnp.cos(ang), jnp.sin(ang)
    zero = jnp.zeros((L, 128 - ROPE_DIM), F32)
    return jnp.concatenate([cos, cos, zero], axis=1), jnp.concatenate([-sin, sin, zero], axis=1)


def _swap_halves(w):
    half = w.shape[-1] // 2
    return jnp.concatenate([w[..., half:], w[..., :half]], axis=-1)


def _layer_weights(p, i):
    w_in = p["w_in"][i]
    s0, s1, s2, s3, s4, s5 = (Q_LORA, Q_LORA + KV_LORA, Q_LORA + KV_LORA + ROPE_DIM,
                              Q_LORA + KV_LORA + ROPE_DIM + SSD_INNER,
                              Q_LORA + KV_LORA + ROPE_DIM + SSD_INNER + CONV_DIM,
                              Q_LORA + KV_LORA + ROPE_DIM + SSD_INNER + CONV_DIM + 2 * SSD_HEADS)
    w_kr = w_in[:, s1:s2]
    zcol = lambda n: jnp.zeros((D_MODEL, n), F32)
    wcat = jnp.concatenate(
        [w_in[:, :s0], w_in[:, s0:s1], w_in[:, s2:s3], w_in[:, s3:s4], w_in[:, s5:],
         w_kr, w_in[:, s4:s5], zcol(128 - ROPE_DIM - 2 * SSD_HEADS),
         _swap_halves(w_kr), zcol(128 - ROPE_DIM)], axis=1).astype(BF16)
    wqb = p["w_q_b"][i].reshape(Q_LORA, ATTN_HEADS, NOPE_DIM + ROPE_DIM)
    zq = jnp.zeros((Q_LORA, ATTN_HEADS, 128 - ROPE_DIM), F32)
    wq = jnp.concatenate([wqb, zq], axis=2).reshape(Q_LORA, ATTN_HEADS * QK_PAD).astype(BF16)
    wqs = jnp.concatenate([_swap_halves(wqb[:, :, NOPE_DIM:]), zq], axis=2).reshape(
        Q_LORA, ATTN_HEADS * 128).astype(BF16)
    wkv = p["w_kv_b"][i].reshape(KV_LORA, ATTN_HEADS, NOPE_DIM + V_DIM)
    wk = wkv[:, :, :NOPE_DIM].reshape(KV_LORA, ATTN_HEADS * NOPE_DIM).astype(BF16)
    wv = wkv[:, :, NOPE_DIM:].reshape(KV_LORA, ATTN_HEADS * V_DIM).T.astype(BF16)
    lane_pad = lambda f, b: jnp.concatenate(
        [jnp.zeros((DT_LANE,), F32), f, b, jnp.zeros((128 - DT_LANE - 2 * SSD_HEADS,), F32)])[None, :]
    w_out = p["w_out"][i].astype(BF16)
    return dict(
        n1g=p["norm1_g"][i][None, :], n2g=p["norm2_g"][i][None, :],
        wcat=wcat, qg=p["q_a_norm_g"][i][None, :], wq=wq, wqs=wqs,
        kvg=p["kv_a_norm_g"][i][None, :], wk=wk, wv=wv,
        conv_w=p["conv_w"][i], conv_b=p["conv_b"][i][None, :],
        dt_bias=lane_pad(p["dt_bias_fwd"][i], p["dt_bias_bwd"][i]),
        a_neg=lane_pad(-jnp.exp(p["a_log_fwd"][i]), -jnp.exp(p["a_log_bwd"][i])),
        d_skip=jnp.repeat(p["d_skip"][i], SSD_HEAD_DIM)[None, :], ssd_g=p["ssd_norm_g"][i][None, :],
        pool_w=p["pool_w"][i].astype(BF16), pool_scale=p["pool_scale"][i][None, :],
        wo_a=w_out[:ATTN_OUT], wo_s=w_out[ATTN_OUT:ATTN_OUT + SSD_INNER], wo_p=w_out[ATTN_OUT + SSD_INNER:],
        wpq=p["peer_wq"][i].astype(BF16),
        keys=p["peer_keys"][i].reshape(PEER_HEADS * 2, N_KEYS, PEER_HALF).astype(BF16),
    )


def _block(n, pref):
    for c in pref:
        if n % c == 0:
            return c
    raise ValueError(f"no block size in {pref} divides {n}")


def _encoder(x, mods, weights, experts, fg):
    B, L, _ = x.shape
    cos_t, sin_t = _rope_tables(L)
    tm = _block(L, (256, 128))
    tq = _block(L, (512, 256, 128))
    tk = _block(L // ATTN_SLOTS, (1024, 512, 256, 128))
    nchunk = _block(L // SSD_CHUNK, (4, 2, 1))
    lp = _block(L, (512, 256, 128))
    tb = _block(L, (1024, 512, 256, 128))
    tt = _block(B * L, (512, 256, 128))
    tg = _block(B * L, (64, 32))
    peer = None
    for i in range(DEPTH):
        w, mod = weights[i], mods[i]
        if peer is None:
            q, k, v, z, xbc, pool_in, kd = _pre_mixer(x, mod, w, cos_t, sin_t, tm)
        else:
            q, k, v, z, xbc, pool_in, kd, x = _pre_mixer(x1, mod, w, cos_t, sin_t, tm, peer, mods[i - 1])
        attn = _attention(q, k, v, tq, tk)
        yf = _ssd_pass(xbc, kd, w, nchunk, reverse=False)
        ssd = _ssd_pass(xbc, kd, w, nchunk, reverse=True, yf=yf, z=z)
        pool = _pool(pool_in, w, lp)
        x1, h2, qp = _post_mixer(x, mod, attn, ssd, pool, w, tm)
        e, g = _peer_topk(qp.reshape(B * L, D_MODEL), w["keys"], tt)
        gates = _peer_gates(e, g, tg).reshape(B, L, N_EXPERTS)
        peer = _peer_dense(h2, experts[0], experts[1], gates, tb, 1024, i)
    return _final_norm(x1, peer, mods[DEPTH - 1], fg, tm)


def kernel(x_prompt, x_sample, c_prompt, c_sample, mod_w, mod_b, norm1_g, norm2_g, w_in, q_a_norm_g, w_q_b, kv_a_norm_g, w_kv_b, conv_w, conv_b, a_log_fwd, a_log_bwd, dt_bias_fwd, dt_bias_bwd, d_skip, ssd_norm_g, pool_w, pool_scale, w_out, peer_wq, peer_keys, peer_u, peer_v, final_norm_g):
    p = dict(mod_w=mod_w, mod_b=mod_b, norm1_g=norm1_g, norm2_g=norm2_g, w_in=w_in,
             q_a_norm_g=q_a_norm_g, w_q_b=w_q_b, kv_a_norm_g=kv_a_norm_g, w_kv_b=w_kv_b,
             conv_w=conv_w, conv_b=conv_b, a_log_fwd=a_log_fwd, a_log_bwd=a_log_bwd,
             dt_bias_fwd=dt_bias_fwd, dt_bias_bwd=dt_bias_bwd, d_skip=d_skip, ssd_norm_g=ssd_norm_g,
             pool_w=pool_w, pool_scale=pool_scale, w_out=w_out, peer_wq=peer_wq, peer_keys=peer_keys,
             peer_u=peer_u, peer_v=peer_v)
    weights = [_layer_weights(p, i) for i in range(DEPTH)]
    bp, bs = c_prompt.shape[0], c_sample.shape[0]
    c_pad = jnp.concatenate([c_prompt, c_sample, jnp.zeros((8 - bp - bs, D_MODEL), F32)], axis=0)
    mods = [_modulation(c_pad, mod_w, mod_b[i][None, :], i) for i in range(DEPTH)]
    fg = final_norm_g[None, :]
    mods_p = [m[:bp].reshape(bp, 6, D_MODEL) for m in mods]
    mods_s = [m[bp:bp + bs].reshape(bs, 6, D_MODEL) for m in mods]
    experts = (peer_u.astype(BF16), peer_v.astype(BF16))
    return (_encoder(x_prompt, mods_p, weights, experts, fg), _encoder(x_sample, mods_s, weights, experts, fg))
```

```python
import functools
import math

import jax
import jax.numpy as jnp
from jax import lax
from jax.experimental import pallas as pl
from jax.experimental.pallas import tpu as pltpu

F32 = jnp.float32
BF16 = jnp.bfloat16
I32 = jnp.int32

D_MODEL = 2048
DEPTH = 2
EPS = 1e-6
ATTN_HEADS = 8
Q_LORA = 512
KV_LORA = 256
NOPE_DIM = 128
ROPE_DIM = 64
V_DIM = 128
ROPE_BASE = 10000.0
QK_PAD = 256
SSD_HEADS = 8
SSD_HEAD_DIM = 64
SSD_INNER = SSD_HEADS * SSD_HEAD_DIM
SSD_GROUPS = 2
SSD_STATE = 128
SSD_CHUNK = 128
D_CONV = 4
CONV_DIM = SSD_INNER + 2 * SSD_GROUPS * SSD_STATE
POOL_WINDOWS = (2, 4, 8, 16)
POOL_GROUP = 128
POOL_DIM = len(POOL_WINDOWS) * POOL_GROUP
ATTN_OUT = ATTN_HEADS * V_DIM
PEER_HEADS = 8
N_KEYS = 128
N_EXPERTS = N_KEYS * N_KEYS
PEER_HALF = 128
PEER_TOPK = 16
HK = PEER_HEADS * PEER_TOPK
HALO = 8
DT_LANE = 64

C_CQ = 0
C_CKV = C_CQ + Q_LORA
C_Z = C_CKV + KV_LORA
C_XBC = C_Z + SSD_INNER
C_POOL = C_XBC + CONV_DIM
C_KD = C_POOL + POOL_DIM
C_END = C_KD + 128

VMEM_LIMIT = 56 * 1024 * 1024

_STAIR = [(a, PEER_TOPK // (a + 1)) for a in range(PEER_TOPK)]
_STAIR_ROWS = sum(n for _, n in _STAIR)
_STAIR_PAD = ((_STAIR_ROWS + 7) // 8) * 8


def _cparams(*sem):
    return pltpu.CompilerParams(dimension_semantics=sem, vmem_limit_bytes=VMEM_LIMIT)


def _const_spec(shape):
    nd = len(shape)
    return pl.BlockSpec(shape, lambda *_: (0,) * nd, pipeline_mode=pl.Buffered(1))


def _rms(x, g):
    return x * lax.rsqrt(jnp.mean(x * x, axis=-1, keepdims=True) + EPS) * g


def _silu(x):
    return x * jax.nn.sigmoid(x)


def _gelu(x):
    return 0.5 * x * (1.0 + lax.erf(x * (1.0 / math.sqrt(2.0))))


def _dot(a, b):
    return jnp.dot(a, b, preferred_element_type=F32)


def _dot_nt(a, b):
    return lax.dot_general(a, b, (((1,), (1,)), ((), ())), preferred_element_type=F32)


def _mod_kernel(c_ref, w_ref, b_ref, o_ref):
    cs = _silu(c_ref[...])
    o_ref[...] = _dot(cs.astype(BF16), w_ref[0].astype(BF16)) + b_ref[...]


def _modulation(c_pad, mod_w, mod_b, layer):
    bn = 1024
    n = mod_w.shape[2]
    return pl.pallas_call(
        _mod_kernel,
        grid=(n // bn,),
        in_specs=[pl.BlockSpec((8, D_MODEL), lambda j: (0, 0)),
                  pl.BlockSpec((1, D_MODEL, bn), lambda j: (layer, 0, j)),
                  pl.BlockSpec((1, bn), lambda j: (0, j))],
        out_specs=pl.BlockSpec((8, bn), lambda j: (0, j)),
        out_shape=jax.ShapeDtypeStruct((8, n), F32),
        compiler_params=_cparams("parallel"),
        name="modulation",
    )(c_pad, mod_w, mod_b)


def _pre_kernel(*refs, residual):
    if residual:
        pe_ref, pmod_ref, x_ref, *refs = refs
    else:
        x_ref, *refs = refs
    (mod_ref, n1g_ref, wcat_ref, qg_ref, wq_ref, kvg_ref, wk_ref, wv_ref, cos_ref, sin_ref,
     q_ref, k_ref, v_ref, z_ref, xbc_ref, pool_ref, kd_ref, *xo_ref) = refs
    x = x_ref[0]
    if residual:
        x = x + pmod_ref[0, 5:6, :] * pe_ref[0]
        xo_ref[0][0] = x
    sh1 = mod_ref[0, 0:1, :]
    sc1 = mod_ref[0, 1:2, :]
    h = _rms(x, n1g_ref[...]) * (1.0 + sc1) + sh1
    proj = _dot(h.astype(BF16), wcat_ref[...])
    z_ref[0] = proj[:, C_Z:C_XBC]
    xbc_ref[0] = proj[:, C_XBC:C_POOL]
    pool_ref[0] = proj[:, C_POOL:C_KD]
    kd = proj[:, C_KD:C_END]
    kd_ref[0] = kd
    cos = cos_ref[...]
    sin = sin_ref[...]
    scale = math.log2(math.e) / math.sqrt(NOPE_DIM + ROPE_DIM)
    first_half = lax.broadcasted_iota(I32, kd.shape, 1) < ROPE_DIM // 2

    def roped(t):
        swapped = jnp.where(first_half, pltpu.roll(t, 128 - ROPE_DIM // 2, 1), pltpu.roll(t, ROPE_DIM // 2, 1))
        return t * cos + swapped * sin

    cqn = _rms(proj[:, C_CQ:C_CKV], qg_ref[...]).astype(BF16)
    qm = _dot(cqn, wq_ref[...])
    for hd in range(ATTN_HEADS):
        o = hd * QK_PAD
        q_ref[0, hd, :, 0:NOPE_DIM] = (qm[:, o:o + NOPE_DIM] * scale).astype(BF16)
        q_ref[0, hd, :, NOPE_DIM:QK_PAD] = (roped(qm[:, o + NOPE_DIM:o + QK_PAD]) * scale).astype(BF16)

    ckvn = _rms(proj[:, C_CKV:C_Z], kvg_ref[...]).astype(BF16)
    kn = _dot(ckvn, wk_ref[...])
    vvt = _dot_nt(wv_ref[...], ckvn)
    krope = roped(kd).astype(BF16)
    for hd in range(ATTN_HEADS):
        k_ref[0, hd, :, 0:NOPE_DIM] = kn[:, hd * NOPE_DIM:(hd + 1) * NOPE_DIM].astype(BF16)
        k_ref[0, hd, :, NOPE_DIM:QK_PAD] = krope
        v_ref[0, hd] = vvt[hd * V_DIM:(hd + 1) * V_DIM, :].astype(BF16)


def _pre_mixer(x, mod, w, cos_t, sin_t, tm, peer=None, prev_mod=None):
    B, L, _ = x.shape
    grid = (B, L // tm)
    tok = lambda b, i: (b, i, 0)
    head = lambda b, i: (b, 0, i, 0)
    modspec = pl.BlockSpec((1, 6, D_MODEL), lambda b, i: (b, 0, 0))
    residual = peer is not None
    res_specs = [pl.BlockSpec((1, tm, D_MODEL), tok), modspec] if residual else []
    res_args = [peer, prev_mod] if residual else []
    res_out_specs = [pl.BlockSpec((1, tm, D_MODEL), tok)] if residual else []
    res_out_shape = [jax.ShapeDtypeStruct((B, L, D_MODEL), F32)] if residual else []
    return pl.pallas_call(
        functools.partial(_pre_kernel, residual=residual),
        grid=grid,
        in_specs=res_specs + [
                  pl.BlockSpec((1, tm, D_MODEL), tok),
                  modspec,
                  _const_spec((1, D_MODEL)),
                  _const_spec((D_MODEL, C_END)),
                  _const_spec((1, Q_LORA)),
                  _const_spec((Q_LORA, ATTN_HEADS * QK_PAD)),
                  _const_spec((1, KV_LORA)),
                  _const_spec((KV_LORA, ATTN_HEADS * NOPE_DIM)),
                  _const_spec((ATTN_HEADS * V_DIM, KV_LORA)),
                  pl.BlockSpec((tm, 128), lambda b, i: (i, 0)),
                  pl.BlockSpec((tm, 128), lambda b, i: (i, 0))],
        out_specs=[pl.BlockSpec((1, ATTN_HEADS, tm, QK_PAD), head),
                   pl.BlockSpec((1, ATTN_HEADS, tm, QK_PAD), head),
                   pl.BlockSpec((1, ATTN_HEADS, V_DIM, tm), lambda b, i: (b, 0, 0, i)),
                   pl.BlockSpec((1, tm, SSD_INNER), tok),
                   pl.BlockSpec((1, tm, CONV_DIM), tok),
                   pl.BlockSpec((1, tm, POOL_DIM), tok),
                   pl.BlockSpec((1, tm, 128), tok)] + res_out_specs,
        out_shape=[jax.ShapeDtypeStruct((B, ATTN_HEADS, L, QK_PAD), BF16),
                   jax.ShapeDtypeStruct((B, ATTN_HEADS, L, QK_PAD), BF16),
                   jax.ShapeDtypeStruct((B, ATTN_HEADS, V_DIM, L), BF16),
                   jax.ShapeDtypeStruct((B, L, SSD_INNER), F32),
                   jax.ShapeDtypeStruct((B, L, CONV_DIM), F32),
                   jax.ShapeDtypeStruct((B, L, POOL_DIM), F32),
                   jax.ShapeDtypeStruct((B, L, 128), F32)] + res_out_shape,
        compiler_params=_cparams("parallel", "parallel"),
        name="pre_mixer",
    )(*res_args, x, mod, w["n1g"], w["wcat"], w["qg"], w["wq"], w["kvg"], w["wk"], w["wv"], cos_t, sin_t)


ATTN_SLOTS = 4


def _attn_kernel(q_ref, k_ref, vt_ref, o_ref, st_sc, p_sc, *, tk, nk):
    q = q_ref[0, 0]
    tq = q.shape[0]
    ntrip = nk // ATTN_SLOTS

    def scores(j):
        st = _dot_nt(k_ref[0, 0, j * tk:(j + 1) * tk, :], q)
        return st, jnp.max(st, axis=0, keepdims=True)

    def weighted_values(j, slot):
        return _dot(vt_ref[0, 0, :, j * tk:(j + 1) * tk], p_sc[slot])

    def trip(jj, carry, first, last):
        for u in range(ATTN_SLOTS):
            j = ATTN_SLOTS * jj + u
            m_prev, l_prev, acc, a_prev, cmax, cmax_1 = carry
            if not (first and u == 0):
                acc = a_prev * acc + weighted_values(j - 1, (u - 1) % 2)
            if last and u >= ATTN_SLOTS - 2:
                cmax_2 = cmax_1
            else:
                st, cmax_2 = scores(j + 2)
                st_sc[(u + 2) % ATTN_SLOTS] = st
            m_new = jnp.maximum(m_prev, cmax)
            alpha = jnp.exp2(m_prev - m_new)
            p = jnp.exp2(st_sc[u] - m_new)
            l_new = alpha * l_prev + jnp.sum(p, axis=0, keepdims=True)
            p_sc[u % 2] = p.astype(BF16)
            carry = (m_new, l_new, acc, alpha, cmax_1, cmax_2)
        return carry

    st_sc[0], cmax0 = scores(0)
    st_sc[1], cmax1 = scores(1)
    carry = (jnp.full((1, tq), -jnp.inf, F32), jnp.zeros((1, tq), F32), jnp.zeros((V_DIM, tq), F32),
             jnp.ones((1, tq), F32), cmax0, cmax1)
    carry = trip(0, carry, True, ntrip == 1)
    for jj in range(1, ntrip - 1):
        carry = trip(jj, carry, False, False)
    if ntrip > 1:
        carry = trip(ntrip - 1, carry, False, True)
    _, l, acc, a_last, _, _ = carry
    acc = a_last * acc + weighted_values(nk - 1, (ATTN_SLOTS - 1) % 2)
    o_ref[0] = (acc / l).T.astype(o_ref.dtype)


def _attention(q, k, vt, tq, tk):
    B, H, L, _ = q.shape
    nk = L // tk
    assert nk % ATTN_SLOTS == 0, (L, tk)
    kern = functools.partial(_attn_kernel, tk=tk, nk=nk)
    return pl.pallas_call(
        kern,
        grid=(B, H, L // tq),
        in_specs=[pl.BlockSpec((1, 1, tq, QK_PAD), lambda b, h, i: (b, h, i, 0)),
                  pl.BlockSpec((1, 1, L, QK_PAD), lambda b, h, i: (b, h, 0, 0)),
                  pl.BlockSpec((1, 1, V_DIM, L), lambda b, h, i: (b, h, 0, 0))],
        out_specs=pl.BlockSpec((1, tq, V_DIM), lambda b, h, i: (b, i, h)),
        out_shape=jax.ShapeDtypeStruct((B, L, H * V_DIM), BF16),
        scratch_shapes=[pltpu.VMEM((ATTN_SLOTS, tk, tq), F32), pltpu.VMEM((2, tk, tq), BF16)],
        compiler_params=_cparams("parallel", "parallel", "arbitrary"),
        name="attention",
    )(q, k, vt)


def _halo_ext(prev_ref, cur_ref, next_ref, i, nblk):
    prev = jnp.where(i > 0, prev_ref[0], 0.0)
    nxt = jnp.where(i < nblk - 1, next_ref[0], 0.0)
    return jnp.concatenate([prev, cur_ref[0], nxt], axis=0)


def _shift_rows(ext, d, lb):
    n = ext.shape[0]
    r = ext if d == 0 else pltpu.roll(ext, (-d) % n, 0)
    return r[HALO:HALO + lb]


def _ssd_kernel(*refs, reverse, final, nblk, nchunk):
    if final:
        (xp_ref, xc_ref, xn_ref, kd_ref, cw_ref, cb_ref, dtb_ref, a_ref, yf_ref, z_ref, dsk_ref, ng_ref,
         o_ref, st_sc) = refs
    else:
        xp_ref, xc_ref, xn_ref, kd_ref, cw_ref, cb_ref, dtb_ref, a_ref, o_ref, st_sc = refs
    step = pl.program_id(1)
    blk = (nblk - 1 - step) if reverse else step
    lb = nchunk * SSD_CHUNK

    @pl.when(step == 0)
    def _():
        st_sc[...] = jnp.zeros(st_sc.shape, F32)

    ext = _halo_ext(xp_ref, xc_ref, xn_ref, blk, nblk)
    conv = cb_ref[...] + sum(cw_ref[kk:kk + 1, :] * _shift_rows(ext, kk - D_CONV // 2, lb) for kk in range(D_CONV))
    act = _silu(conv)
    dt_all = jax.nn.softplus(kd_ref[0] + dtb_ref[...])
    da_all = dt_all * a_ref[...]

    row = lax.broadcasted_iota(I32, (SSD_CHUNK, SSD_CHUNK), 0)
    col = lax.broadcasted_iota(I32, (SSD_CHUNK, SSD_CHUNK), 1)
    keep = (row <= col) if reverse else (row >= col)
    tri = jnp.where(keep, 1.0, 0.0).astype(F32)
    lane = lax.broadcasted_iota(I32, (SSD_CHUNK, 128), 1)
    low = lane < SSD_HEAD_DIM
    dlane = DT_LANE + (SSD_HEADS if reverse else 0)
    edge = 0 if reverse else SSD_CHUNK - 1

    def pair(arr, h0):
        return jnp.where(low, arr[:, dlane + h0:dlane + h0 + 1], arr[:, dlane + h0 + 1:dlane + h0 + 2])

    ys = []
    for cc in (range(nchunk - 1, -1, -1) if reverse else range(nchunk)):
        r0 = cc * SSD_CHUNK
        da = da_all[r0:r0 + SSD_CHUNK]
        dt = dt_all[r0:r0 + SSD_CHUNK]
        acum = jnp.dot(tri, da, preferred_element_type=F32, precision=lax.Precision.HIGHEST)
        acum_t = acum.T
        a_edge = acum[edge:edge + 1, :]
        ycols = [None] * (SSD_HEADS // 2)
        for g in range(SSD_GROUPS):
            bg = act[r0:r0 + SSD_CHUNK, SSD_INNER + g * SSD_STATE:SSD_INNER + (g + 1) * SSD_STATE]
            cg = act[r0:r0 + SSD_CHUNK, SSD_INNER + (SSD_GROUPS + g) * SSD_STATE:
                     SSD_INNER + (SSD_GROUPS + g + 1) * SSD_STATE]
            cgb = cg.astype(BF16)
            cb = _dot_nt(cgb, bg.astype(BF16))
            bgt = bg.T.astype(BF16)
            for pp in range(SSD_HEADS // SSD_GROUPS // 2):
                pi = g * 2 + pp
                h0 = 2 * pi
                xdt = act[r0:r0 + SSD_CHUNK, pi * 128:(pi + 1) * 128] * pair(dt, h0)
                xdtb = xdt.astype(BF16)
                yd = []
                for hh in (h0, h0 + 1):
                    seg = acum[:, dlane + hh:dlane + hh + 1] - acum_t[dlane + hh:dlane + hh + 1, :]
                    decay = jnp.exp(jnp.where(keep, seg, -jnp.inf))
                    yd.append(_dot((cb * decay).astype(BF16), xdtb))
                y_diag = jnp.where(low, yd[0], yd[1])
                st = st_sc[pi]
                y_off = _dot(cgb, st.astype(BF16)) * jnp.exp(pair(acum, h0))
                ycols[pi] = y_diag + y_off
                to_edge = jnp.exp(pair(a_edge - acum, h0))
                st_sc[pi] = st * jnp.exp(pair(a_edge, h0)) + _dot(bgt, (xdt * to_edge).astype(BF16))
        ys.append((cc, jnp.concatenate(ycols, axis=1)))
    y = jnp.concatenate([v for _, v in sorted(ys, key=lambda t: t[0])], axis=0)

    if final:
        tot = yf_ref[0] + y + act[:, 0:SSD_INNER] * dsk_ref[...]
        o_ref[0] = _rms(tot * _silu(z_ref[0]), ng_ref[...]).astype(o_ref.dtype)
    else:
        o_ref[0] = y


def _ssd_pass(xbc, kd, w, nchunk, reverse, yf=None, z=None):
    B, L, _ = xbc.shape
    lb = nchunk * SSD_CHUNK
    nblk = L // lb
    hb = lb // HALO
    final = yf is not None
    bi = (lambda i: nblk - 1 - i) if reverse else (lambda i: i)
    cur = lambda b, i: (b, bi(i), 0)
    prv = lambda b, i: (b, jnp.maximum(bi(i) * hb - 1, 0), 0)
    nxt = lambda b, i: (b, jnp.minimum((bi(i) + 1) * hb, L // HALO - 1), 0)
    in_specs = [pl.BlockSpec((1, HALO, CONV_DIM), prv),
                pl.BlockSpec((1, lb, CONV_DIM), cur),
                pl.BlockSpec((1, HALO, CONV_DIM), nxt),
                pl.BlockSpec((1, lb, 128), cur),
                _const_spec((D_CONV, CONV_DIM)), _const_spec((1, CONV_DIM)),
                _const_spec((1, 128)), _const_spec((1, 128))]
    args = [xbc, xbc, xbc, kd, w["conv_w"], w["conv_b"], w["dt_bias"], w["a_neg"]]
    if final:
        in_specs += [pl.BlockSpec((1, lb, SSD_INNER), cur), pl.BlockSpec((1, lb, SSD_INNER), cur),
                     _const_spec((1, SSD_INNER)), _const_spec((1, SSD_INNER))]
        args += [yf, z, w["d_skip"], w["ssd_g"]]
    kern = functools.partial(_ssd_kernel, reverse=reverse, final=final, nblk=nblk, nchunk=nchunk)
    return pl.pallas_call(
        kern,
        grid=(B, nblk),
        in_specs=in_specs,
        out_specs=pl.BlockSpec((1, lb, SSD_INNER), cur),
        out_shape=jax.ShapeDtypeStruct((B, L, SSD_INNER), BF16 if final else F32),
        scratch_shapes=[pltpu.VMEM((SSD_HEADS // 2, SSD_STATE, 128), F32)],
        compiler_params=_cparams("parallel", "arbitrary"),
        name="ssd_bwd" if reverse else "ssd_fwd",
    )(*args)


def _pool_kernel(up_ref, uc_ref, un_ref, pw_ref, ps_ref, o_ref, *, nblk, lb, seq):
    blk = pl.program_id(1)
    ext = _halo_ext(up_ref, uc_ref, un_ref, blk, nblk)
    n = ext.shape[0]
    t = blk * lb + lax.broadcasted_iota(I32, (lb, 1), 0)
    for gi, wdw in enumerate(POOL_WINDOWS):
        e = ext[:, gi * POOL_GROUP:(gi + 1) * POOL_GROUP]
        run, width = e, 1
        while width < wdw:
            run = run + pltpu.roll(run, width, 0)
            width *= 2
        tot = _shift_rows(run, wdw // 2 - 1, lb)
        cnt = jnp.minimum(t + wdw // 2, seq) - jnp.maximum(t - wdw // 2, 0)
        d = tot / cnt.astype(F32) - e[HALO:HALO + lb]
        y = _dot(d.astype(BF16), pw_ref[gi])
        o_ref[0, :, gi * POOL_GROUP:(gi + 1) * POOL_GROUP] = (
            y * ps_ref[:, gi * POOL_GROUP:(gi + 1) * POOL_GROUP]).astype(o_ref.dtype)


def _pool(u, w, lb):
    B, L, _ = u.shape
    nblk = L // lb
    hb = lb // HALO
    cur = lambda b, i: (b, i, 0)
    prv = lambda b, i: (b, jnp.maximum(i * hb - 1, 0), 0)
    nxt = lambda b, i: (b, jnp.minimum((i + 1) * hb, L // HALO - 1), 0)
    kern = functools.partial(_pool_kernel, nblk=nblk, lb=lb, seq=L)
    return pl.pallas_call(
        kern,
        grid=(B, nblk),
        in_specs=[pl.BlockSpec((1, HALO, POOL_DIM), prv),
                  pl.BlockSpec((1, lb, POOL_DIM), cur),
                  pl.BlockSpec((1, HALO, POOL_DIM), nxt),
                  _const_spec((len(POOL_WINDOWS), POOL_GROUP, POOL_GROUP)),
                  _const_spec((1, POOL_DIM))],
        out_specs=pl.BlockSpec((1, lb, POOL_DIM), cur),
        out_shape=jax.ShapeDtypeStruct((B, L, POOL_DIM), BF16),
        compiler_params=_cparams("parallel", "parallel"),
        name="pool",
    )(u, u, u, w["pool_w"], w["pool_scale"])


def _post_kernel(x_ref, mod_ref, at_ref, sd_ref, po_ref, woa_ref, wos_ref, wop_ref, n2g_ref, wpq_ref,
                 x1_ref, h2_ref, qp_ref):
    g1 = mod_ref[0, 2:3, :]
    sh2 = mod_ref[0, 3:4, :]
    sc2 = mod_ref[0, 4:5, :]
    mix = _dot(at_ref[0], woa_ref[...]) + _dot(sd_ref[0], wos_ref[...]) + _dot(po_ref[0], wop_ref[...])
    x1 = x_ref[0] + g1 * mix
    x1_ref[0] = x1
    h2 = (_rms(x1, n2g_ref[...]) * (1.0 + sc2) + sh2).astype(BF16)
    h2_ref[0] = h2
    qp_ref[0] = _dot(h2, wpq_ref[...]).astype(BF16)


def _post_mixer(x, mod, attn, ssd, pool, w, tm):
    B, L, _ = x.shape
    tok = lambda b, i: (b, i, 0)
    return pl.pallas_call(
        _post_kernel,
        grid=(B, L // tm),
        in_specs=[pl.BlockSpec((1, tm, D_MODEL), tok),
                  pl.BlockSpec((1, 6, D_MODEL), lambda b, i: (b, 0, 0)),
                  pl.BlockSpec((1, tm, ATTN_OUT), tok),
                  pl.BlockSpec((1, tm, SSD_INNER), tok),
                  pl.BlockSpec((1, tm, POOL_DIM), tok),
                  _const_spec((ATTN_OUT, D_MODEL)), _const_spec((SSD_INNER, D_MODEL)),
                  _const_spec((POOL_DIM, D_MODEL)), _const_spec((1, D_MODEL)),
                  _const_spec((D_MODEL, D_MODEL))],
        out_specs=[pl.BlockSpec((1, tm, D_MODEL), tok)] * 3,
        out_shape=[jax.ShapeDtypeStruct((B, L, D_MODEL), F32),
                   jax.ShapeDtypeStruct((B, L, D_MODEL), BF16),
                   jax.ShapeDtypeStruct((B, L, D_MODEL), BF16)],
        compiler_params=_cparams("parallel", "parallel"),
        name="post_mixer",
    )(x, mod, attn, ssd, pool, w["wo_a"], w["wo_s"], w["wo_p"], w["n2g"], w["wpq"])


def _top16(s, out_v, out_i):
    n = s.shape[0]
    rows = lax.broadcasted_iota(I32, s.shape, 0).astype(F32)
    for r in range(PEER_TOPK):
        m = jnp.max(s, axis=0, keepdims=True)
        idx = jnp.min(jnp.where(s == m, rows, float(n)), axis=0, keepdims=True)
        out_v[r:r + 1, :] = m
        out_i[r:r + 1, :] = idx
        s = jnp.where(rows == idx, -jnp.inf, s)


def _topk_kernel(qp_ref, keys_ref, e_ref, g_ref, v1_sc, i1_sc, v2_sc, i2_sc, cv_sc, ci_sc, tv_sc, ti_sc,
                 eo_sc, go_sc):
    cv_sc[...] = jnp.full(cv_sc.shape, -jnp.inf, F32)
    ci_sc[...] = jnp.zeros(ci_sc.shape, F32)
    for hd in range(PEER_HEADS):
        for half, (vs, is_) in enumerate(((v1_sc, i1_sc), (v2_sc, i2_sc))):
            c0 = (hd * 2 + half) * PEER_HALF
            s = _dot_nt(keys_ref[hd * 2 + half], qp_ref[:, c0:c0 + PEER_HALF])
            _top16(s, vs, is_)
        off = 0
        for a, nb in _STAIR:
            cv_sc[off:off + nb, :] = v1_sc[a:a + 1, :] + v2_sc[0:nb, :]
            ci_sc[off:off + nb, :] = i1_sc[a:a + 1, :] * float(N_KEYS) + i2_sc[0:nb, :]
            off += nb
        cand = cv_sc[...]
        cidx = ci_sc[...]
        rows = lax.broadcasted_iota(I32, cand.shape, 0).astype(F32)
        for r in range(PEER_TOPK):
            m = jnp.max(cand, axis=0, keepdims=True)
            pos = jnp.min(jnp.where(cand == m, rows, float(_STAIR_PAD)), axis=0, keepdims=True)
            hit = rows == pos
            tv_sc[r:r + 1, :] = m
            ti_sc[r:r + 1, :] = jnp.sum(jnp.where(hit, cidx, 0.0), axis=0, keepdims=True)
            cand = jnp.where(hit, -jnp.inf, cand)
        tv = tv_sc[...]
        p = jnp.exp(tv - tv[0:1, :])
        go_sc[hd * PEER_TOPK:(hd + 1) * PEER_TOPK, :] = p / jnp.sum(p, axis=0, keepdims=True)
        eo_sc[hd * PEER_TOPK:(hd + 1) * PEER_TOPK, :] = ti_sc[...]
    e_ref[...] = eo_sc[...].T.astype(I32)
    g_ref[...] = go_sc[...].T


def _peer_topk(qp, keys, tm):
    T = qp.shape[0]
    return pl.pallas_call(
        _topk_kernel,
        grid=(T // tm,),
        in_specs=[pl.BlockSpec((tm, D_MODEL), lambda i: (i, 0)),
                  _const_spec((PEER_HEADS * 2, N_KEYS, PEER_HALF))],
        out_specs=[pl.BlockSpec((tm, HK), lambda i: (i, 0))] * 2,
        out_shape=[jax.ShapeDtypeStruct((T, HK), I32), jax.ShapeDtypeStruct((T, HK), F32)],
        scratch_shapes=[pltpu.VMEM((PEER_TOPK, tm), F32)] * 4
                       + [pltpu.VMEM((_STAIR_PAD, tm), F32)] * 2
                       + [pltpu.VMEM((PEER_TOPK, tm), F32)] * 2
                       + [pltpu.VMEM((HK, tm), F32)] * 2,
        compiler_params=_cparams("parallel"),
        name="peer_topk",
    )(qp, keys)


G_PITCH = N_KEYS + 8


G_GROUP = 16


def _gate_kernel(e_ref, g_ref, o_ref, gs_sc, *, tg):
    sub = lax.broadcasted_iota(I32, (N_KEYS, HK), 0)
    ngroup = tg // G_GROUP

    def build(n, slot):
        for u in range(G_GROUP):
            t = n * G_GROUP + u
            er = e_ref[pl.ds(t, 1), :]
            gr = g_ref[pl.ds(t, 1), :]
            i1 = lax.shift_right_logical(er, 7)
            i2 = lax.bitwise_and(er, N_KEYS - 1)
            a = jnp.where(sub == i1, 1.0, 0.0).astype(BF16)
            b = jnp.where(sub == i2, gr, 0.0).astype(BF16)
            gs_sc[slot, pl.ds(u * G_PITCH, N_KEYS), :] = _dot_nt(a, b)

    def regroup(n, slot):
        row = n * G_GROUP if isinstance(n, int) else pl.multiple_of(n * G_GROUP, G_GROUP)
        for c in range(N_KEYS):
            lo = gs_sc[slot, pl.ds(c, 8, stride=G_PITCH), :]
            hi = gs_sc[slot, pl.ds(8 * G_PITCH + c, 8, stride=G_PITCH), :]
            o_ref[pl.ds(row, G_GROUP), c * N_KEYS:(c + 1) * N_KEYS] = (
                jnp.concatenate([lo, hi], axis=0).astype(o_ref.dtype))

    def body(nn, carry):
        n = 2 * nn + 1
        build(n, 1)
        regroup(n - 1, 0)
        build(n + 1, 0)
        regroup(n, 1)
        return carry

    build(0, 0)
    lax.fori_loop(0, (ngroup - 1) // 2, body, 0)
    build(ngroup - 1, 1)
    regroup(ngroup - 2, 0)
    regroup(ngroup - 1, 1)


def _peer_gates(e, g, tg):
    T = e.shape[0]
    assert tg % (2 * G_GROUP) == 0, tg
    kern = functools.partial(_gate_kernel, tg=tg)
    return pl.pallas_call(
        kern,
        grid=(T // tg,),
        in_specs=[pl.BlockSpec((tg, HK), lambda i: (i, 0))] * 2,
        out_specs=pl.BlockSpec((tg, N_EXPERTS), lambda i: (i, 0)),
        out_shape=jax.ShapeDtypeStruct((T, N_EXPERTS), BF16),
        scratch_shapes=[pltpu.VMEM((2, G_GROUP * G_PITCH, N_KEYS), F32)],
        compiler_params=_cparams("parallel"),
        name="peer_gates",
    )(e, g)


def _dense_kernel(h_ref, u_ref, v_ref, gt_ref, o_ref):
    @pl.when(pl.program_id(2) == 0)
    def _():
        o_ref[...] = jnp.zeros(o_ref.shape, F32)

    a = _gelu(_dot_nt(h_ref[0], u_ref[0]))
    wgt = (a * gt_ref[0].astype(F32)).astype(BF16)
    o_ref[0] += _dot(wgt, v_ref[0])


def _peer_dense(h2, u, v, gates, tb, eb, layer):
    B, L, _ = h2.shape
    tok = lambda b, i, j: (b, i, 0)
    return pl.pallas_call(
        _dense_kernel,
        grid=(B, L // tb, N_EXPERTS // eb),
        in_specs=[pl.BlockSpec((1, tb, D_MODEL), tok),
                  pl.BlockSpec((1, eb, D_MODEL), lambda b, i, j: (layer, j, 0)),
                  pl.BlockSpec((1, eb, D_MODEL), lambda b, i, j: (layer, j, 0)),
                  pl.BlockSpec((1, tb, eb), lambda b, i, j: (b, i, j))],
        out_specs=pl.BlockSpec((1, tb, D_MODEL), tok),
        out_shape=jax.ShapeDtypeStruct((B, L, D_MODEL), F32),
        compiler_params=_cparams("parallel", "parallel", "arbitrary"),
        name="peer_dense",
    )(h2, u, v, gates)


def _final_kernel(x1_ref, pe_ref, mod_ref, fg_ref, o_ref):
    o_ref[0] = _rms(x1_ref[0] + mod_ref[0, 5:6, :] * pe_ref[0], fg_ref[...])


def _final_norm(x1, peer, mod, fg, tm):
    B, L, _ = x1.shape
    tok = lambda b, i: (b, i, 0)
    return pl.pallas_call(
        _final_kernel,
        grid=(B, L // tm),
        in_specs=[pl.BlockSpec((1, tm, D_MODEL), tok), pl.BlockSpec((1, tm, D_MODEL), tok),
                  pl.BlockSpec((1, 6, D_MODEL), lambda b, i: (b, 0, 0)), _const_spec((1, D_MODEL))],
        out_specs=pl.BlockSpec((1, tm, D_MODEL), tok),
        out_shape=jax.ShapeDtypeStruct((B, L, D_MODEL), F32),
        compiler_params=_cparams("parallel", "parallel"),
        name="final_norm",
    )(x1, peer, mod, fg)


def _rope_tables(L):
    inv = 1.0 / (ROPE_BASE ** (jnp.arange(0, ROPE_DIM, 2, dtype=F32) / ROPE_DIM))
    ang = jnp.arange(L, dtype=F32)[:, None] * inv[None, :]
    cos, sin = jnp.cos(ang), jnp.sin(ang)
    zero = jnp.zeros((L, 128 - ROPE_DIM), F32)
    return jnp.concatenate([cos, cos, zero], axis=1), jnp.concatenate([-sin, sin, zero], axis=1)


def _layer_weights(p, i):
    w_in = p["w_in"][i]
    s0, s1, s2, s3, s4, s5 = (Q_LORA, Q_LORA + KV_LORA, Q_LORA + KV_LORA + ROPE_DIM,
                              Q_LORA + KV_LORA + ROPE_DIM + SSD_INNER,
                              Q_LORA + KV_LORA + ROPE_DIM + SSD_INNER + CONV_DIM,
                              Q_LORA + KV_LORA + ROPE_DIM + SSD_INNER + CONV_DIM + 2 * SSD_HEADS)
    w_kr = w_in[:, s1:s2]
    zcol = lambda n: jnp.zeros((D_MODEL, n), F32)
    wcat = jnp.concatenate(
        [w_in[:, :s0], w_in[:, s0:s1], w_in[:, s2:s3], w_in[:, s3:s4], w_in[:, s5:],
         w_kr, w_in[:, s4:s5], zcol(128 - ROPE_DIM - 2 * SSD_HEADS)], axis=1).astype(BF16)
    wqb = p["w_q_b"][i].reshape(Q_LORA, ATTN_HEADS, NOPE_DIM + ROPE_DIM)
    zq = jnp.zeros((Q_LORA, ATTN_HEADS, 128 - ROPE_DIM), F32)
    wq = jnp.concatenate([wqb, zq], axis=2).reshape(Q_LORA, ATTN_HEADS * QK_PAD).astype(BF16)
    wkv = p["w_kv_b"][i].reshape(KV_LORA, ATTN_HEADS, NOPE_DIM + V_DIM)
    wk = wkv[:, :, :NOPE_DIM].reshape(KV_LORA, ATTN_HEADS * NOPE_DIM).astype(BF16)
    wv = wkv[:, :, NOPE_DIM:].reshape(KV_LORA, ATTN_HEADS * V_DIM).T.astype(BF16)
    lane_pad = lambda f, b: jnp.concatenate(
        [jnp.zeros((DT_LANE,), F32), f, b, jnp.zeros((128 - DT_LANE - 2 * SSD_HEADS,), F32)])[None, :]
    w_out = p["w_out"][i].astype(BF16)
    return dict(
        n1g=p["norm1_g"][i][None, :], n2g=p["norm2_g"][i][None, :],
        wcat=wcat, qg=p["q_a_norm_g"][i][None, :], wq=wq,
        kvg=p["kv_a_norm_g"][i][None, :], wk=wk, wv=wv,
        conv_w=p["conv_w"][i], conv_b=p["conv_b"][i][None, :],
        dt_bias=lane_pad(p["dt_bias_fwd"][i], p["dt_bias_bwd"][i]),
        a_neg=lane_pad(-jnp.exp(p["a_log_fwd"][i]), -jnp.exp(p["a_log_bwd"][i])),
        d_skip=jnp.repeat(p["d_skip"][i], SSD_HEAD_DIM)[None, :], ssd_g=p["ssd_norm_g"][i][None, :],
        pool_w=p["pool_w"][i].astype(BF16), pool_scale=p["pool_scale"][i][None, :],
        wo_a=w_out[:ATTN_OUT], wo_s=w_out[ATTN_OUT:ATTN_OUT + SSD_INNER], wo_p=w_out[ATTN_OUT + SSD_INNER:],
        wpq=p["peer_wq"][i].astype(BF16),
        keys=p["peer_keys"][i].reshape(PEER_HEADS * 2, N_KEYS, PEER_HALF).astype(BF16),
    )


def _block(n, pref):
    for c in pref:
        if n % c == 0:
            return c
    raise ValueError(f"no block size in {pref} divides {n}")


def _encoder(x, mods, weights, experts, fg):
    B, L, _ = x.shape
    cos_t, sin_t = _rope_tables(L)
    tm = _block(L, (256, 128))
    tq = _block(L, (512, 256, 128))
    tk = _block(L // ATTN_SLOTS, (1024, 512, 256, 128))
    nchunk = _block(L // SSD_CHUNK, (4, 2, 1))
    lp = _block(L, (512, 256, 128))
    tb = _block(L, (1024, 512, 256, 128))
    tt = _block(B * L, (512, 256, 128))
    tg = _block(B * L, (64, 32))
    peer = None
    for i in range(DEPTH):
        w, mod = weights[i], mods[i]
        if peer is None:
            q, k, v, z, xbc, pool_in, kd = _pre_mixer(x, mod, w, cos_t, sin_t, tm)
        else:
            q, k, v, z, xbc, pool_in, kd, x = _pre_mixer(x1, mod, w, cos_t, sin_t, tm, peer, mods[i - 1])
        attn = _attention(q, k, v, tq, tk)
        yf = _ssd_pass(xbc, kd, w, nchunk, reverse=False)
        ssd = _ssd_pass(xbc, kd, w, nchunk, reverse=True, yf=yf, z=z)
        pool = _pool(pool_in, w, lp)
        x1, h2, qp = _post_mixer(x, mod, attn, ssd, pool, w, tm)
        e, g = _peer_topk(qp.reshape(B * L, D_MODEL), w["keys"], tt)
        gates = _peer_gates(e, g, tg).reshape(B, L, N_EXPERTS)
        peer = _peer_dense(h2, experts[0], experts[1], gates, tb, 1024, i)
    return _final_norm(x1, peer, mods[DEPTH - 1], fg, tm)


def kernel(x_prompt, x_sample, c_prompt, c_sample, mod_w, mod_b, norm1_g, norm2_g, w_in, q_a_norm_g, w_q_b, kv_a_norm_g, w_kv_b, conv_w, conv_b, a_log_fwd, a_log_bwd, dt_bias_fwd, dt_bias_bwd, d_skip, ssd_norm_g, pool_w, pool_scale, w_out, peer_wq, peer_keys, peer_u, peer_v, final_norm_g):
    p = dict(mod_w=mod_w, mod_b=mod_b, norm1_g=norm1_g, norm2_g=norm2_g, w_in=w_in,
             q_a_norm_g=q_a_norm_g, w_q_b=w_q_b, kv_a_norm_g=kv_a_norm_g, w_kv_b=w_kv_b,
             conv_w=conv_w, conv_b=conv_b, a_log_fwd=a_log_fwd, a_log_bwd=a_log_bwd,
             dt_bias_fwd=dt_bias_fwd, dt_bias_bwd=dt_bias_bwd, d_skip=d_skip, ssd_norm_g=ssd_norm_g,
             pool_w=pool_w, pool_scale=pool_scale, w_out=w_out, peer_wq=peer_wq, peer_keys=peer_keys,
             peer_u=peer_u, peer_v=peer_v)
    weights = [_layer_weights(p, i) for i in range(DEPTH)]
    bp, bs = c_prompt.shape[0], c_sample.shape[0]
    c_pad = jnp.concatenate([c_prompt, c_sample, jnp.zeros((8 - bp - bs, D_MODEL), F32)], axis=0)
    mods = [_modulation(c_pad, mod_w, mod_b[i][None, :], i) for i in range(DEPTH)]
    fg = final_norm_g[None, :]
    mods_p = [m[:bp].reshape(bp, 6, D_MODEL) for m in mods]
    mods_s = [m[bp:bp + bs].reshape(bs, 6, D_MODEL) for m in mods]
    experts = (peer_u.astype(BF16), peer_v.astype(BF16))
    return (_encoder(x_prompt, mods_p, weights, experts, fg), _encoder(x_sample, mods_s, weights, experts, fg))
```
